```python
import math
import jax, jax.numpy as jnp
from jax import lax
import numpy as np

D_MODEL = 1024
BATCH = 1
SEQ = 16384
DEPTH = 4

N_EVEN = (DEPTH + 1) // 2
N_ODD = DEPTH // 2
A_GROUPS = 4
A_CHUNK = 128
A_WIDTH = D_MODEL // 2
A_GROUP_CH = A_WIDTH // A_GROUPS
B_HEADS = 8
B_HEAD_DIM = 64
B_WIDTH = B_HEADS * B_HEAD_DIM
Q_BLOCK = 128
MIX_WIDTH = A_WIDTH + B_WIDTH
IN_COLS = 2 * A_WIDTH + 3 * B_WIDTH + B_HEADS
S5_GROUP_CH = 16
S5_GROUPS = D_MODEL // S5_GROUP_CH
S5_STATE = 64
D_FF = 2816
CONV_W = 3
PLE_DIM = 256
EPS = 1e-6
NEG_INF = -1e30

kernel_name = "hybrid_gmlp_fox_s5_convffn_ple"


def rms_norm(x, g=None):
    xf = x.astype(jnp.float32)
    y = xf * lax.rsqrt(jnp.mean(xf * xf, axis=-1, keepdims=True) + EPS)
    if g is not None:
        y = y * g.astype(jnp.float32)
    return y.astype(x.dtype)


def gmlp_mixer(u, v, v_gain, w_s, b_s):
    bsz, seq = u.shape[0], u.shape[1]
    u = jax.nn.gelu(u)
    v = jax.nn.gelu(v).reshape(bsz, seq, A_GROUPS, A_GROUP_CH)
    v = rms_norm(v, v_gain.reshape(A_GROUPS, A_GROUP_CH))
    v = v.reshape(bsz, seq // A_CHUNK, A_CHUNK, A_GROUPS, A_GROUP_CH)
    tri = jnp.tril(jnp.ones((A_CHUNK, A_CHUNK), dtype=bool))
    w = jnp.where(tri[None], w_s, jnp.zeros_like(w_s))
    sv = jnp.einsum('gts,bnsgc->bntgc', w, v) + b_s.T[None, None, :, :, None]
    return u * sv.reshape(bsz, seq, A_WIDTH)


def fox_attention(q, k, v, f_logit, q_gain, k_gain):
    bsz, seq = q.shape[0], q.shape[1]
    q = rms_norm(q, q_gain)
    k = rms_norm(k, k_gain)
    c = jnp.cumsum(jax.nn.log_sigmoid(f_logit.astype(jnp.float32)), axis=1)
    nb = seq // Q_BLOCK
    qb = q.reshape(bsz, nb, Q_BLOCK, B_HEADS, B_HEAD_DIM).transpose(1, 0, 2, 3, 4)
    cb = c.reshape(bsz, nb, Q_BLOCK, B_HEADS).transpose(1, 0, 2, 3)
    pos_b = jnp.arange(seq, dtype=jnp.int32).reshape(nb, Q_BLOCK)
    kpos = jnp.arange(seq, dtype=jnp.int32)
    ck = c.transpose(0, 2, 1)
    scale = B_HEAD_DIM ** -0.5

    def block(args):
        qi, ci, pi = args
        s = jnp.einsum('bqhd,bkhd->bhqk', qi, k).astype(jnp.float32) * scale
        s = s + ci.transpose(0, 2, 1)[..., None] - ck[:, :, None, :]
        s = jnp.where(kpos[None, :] <= pi[:, None], s, NEG_INF)
        pr = jax.nn.softmax(s, axis=-1)
        return jnp.einsum('bhqk,bkhd->bqhd', pr.astype(v.dtype), v)

    o = lax.map(block, (qb, cb, pos_b))
    return o.transpose(1, 0, 2, 3, 4).reshape(bsz, seq, B_WIDTH)


def _complex_affine_combine(e1, e2):
    a1r, a1i, b1r, b1i = e1
    a2r, a2i, b2r, b2i = e2
    ar = a2r * a1r - a2i * a1i
    ai = a2r * a1i + a2i * a1r
    br = a2r * b1r - a2i * b1i + b2r
    bi = a2r * b1i + a2i * b1r + b2i
    return (ar, ai, br, bi)


def s5_mixer(u, a_re, a_im, log_dt, b_re, b_im, c_re, c_im, d):
    bsz, seq, _ = u.shape
    f32 = jnp.float32
    uf = u.astype(f32).reshape(bsz, seq, S5_GROUPS, S5_GROUP_CH)
    dt = jnp.exp(log_dt.astype(f32))[:, None]
    lr, li = a_re.astype(f32), a_im.astype(f32)
    mag = jnp.exp(lr * dt)
    ab_re, ab_im = mag * jnp.cos(li * dt), mag * jnp.sin(li * dt)
    den = lr * lr + li * li
    nr, ni = ab_re - 1.0, ab_im
    cr = (nr * lr + ni * li) / den
    ci = (ni * lr - nr * li) / den
    br, bi = b_re.astype(f32), b_im.astype(f32)
    bb_re = cr[..., None] * br - ci[..., None] * bi
    bb_im = cr[..., None] * bi + ci[..., None] * br
    bu_re = jnp.einsum('gpc,bsgc->bsgp', bb_re, uf)
    bu_im = jnp.einsum('gpc,bsgc->bsgp', bb_im, uf)
    a_r = jnp.broadcast_to(ab_re, bu_re.shape)
    a_i = jnp.broadcast_to(ab_im, bu_im.shape)
    _, _, xr, xi = lax.associative_scan(_complex_affine_combine, (a_r, a_i, bu_re, bu_im), axis=1)
    y = (jnp.einsum('gcp,bsgp->bsgc', c_re.astype(f32), xr)
         - jnp.einsum('gcp,bsgp->bsgc', c_im.astype(f32), xi)
         + d.astype(f32).reshape(S5_GROUPS, S5_GROUP_CH) * uf)
    return y.reshape(bsz, seq, D_MODEL).astype(u.dtype)


def conv_ffn(x, w_up, conv_w, conv_b, w_down):
    seq = x.shape[1]
    h = x @ w_up
    hp = jnp.pad(h, ((0, 0), (CONV_W - 1, 0), (0, 0)))
    hc = conv_b + conv_w[0] * hp[:, 0:seq]
    for j in range(1, CONV_W):
        hc = hc + conv_w[j] * hp[:, j:j + seq]
    g, up = jnp.split(hc, 2, axis=-1)
    return (jax.nn.silu(g) * up) @ w_down


def setup_inputs(seed: int = 0) -> dict:
    key = jax.random.key(seed)
    ks = jax.random.split(key, 32)
    f32 = jnp.float32

    def nrm(k, shape, scale):
        return scale * jax.random.normal(k, shape, f32)

    a_im_base = jnp.broadcast_to(math.pi * jnp.arange(S5_STATE, dtype=f32), (N_ODD, S5_GROUPS, S5_STATE))
    return {
        "x": nrm(ks[0], (BATCH, SEQ, D_MODEL), 1.0),
        "p": nrm(ks[1], (DEPTH, BATCH, SEQ, PLE_DIM), 1.0),
        "norm_mix": 1.0 + nrm(ks[2], (DEPTH, D_MODEL), 0.05),
        "norm_ffn": 1.0 + nrm(ks[3], (DEPTH, D_MODEL), 0.05),
        "ev_w_in": nrm(ks[4], (N_EVEN, D_MODEL, IN_COLS), D_MODEL ** -0.5),
        "ev_b_fgate": 4.0 + nrm(ks[5], (N_EVEN, B_HEADS), 0.5),
        "ev_q_norm": 1.0 + nrm(ks[6], (N_EVEN, B_HEAD_DIM), 0.05),
        "ev_k_norm": 1.0 + nrm(ks[7], (N_EVEN, B_HEAD_DIM), 0.05),
        "ev_v_norm": 1.0 + nrm(ks[8], (N_EVEN, A_WIDTH), 0.05),
        "ev_w_spatial": nrm(ks[9], (N_EVEN, A_GROUPS, A_CHUNK, A_CHUNK), A_CHUNK ** -0.5),
        "ev_b_spatial": 1.0 + nrm(ks[10], (N_EVEN, A_GROUPS, A_CHUNK), 0.1),
        "ev_w_out": nrm(ks[11], (N_EVEN, MIX_WIDTH, D_MODEL), MIX_WIDTH ** -0.5),
        "od_a_re": -0.5 + nrm(ks[12], (N_ODD, S5_GROUPS, S5_STATE), 0.01),
        "od_a_im": a_im_base + nrm(ks[13], (N_ODD, S5_GROUPS, S5_STATE), 0.01),
        "od_log_dt": jax.random.uniform(ks[14], (N_ODD, S5_GROUPS), f32, math.log(1e-3), math.log(1e-1)),
        "od_b_re": nrm(ks[15], (N_ODD, S5_GROUPS, S5_STATE, S5_GROUP_CH), (2 * S5_GROUP_CH) ** -0.5),
        "od_b_im": nrm(ks[16], (N_ODD, S5_GROUPS, S5_STATE, S5_GROUP_CH), (2 * S5_GROUP_CH) ** -0.5),
        "od_c_re": nrm(ks[17], (N_ODD, S5_GROUPS, S5_GROUP_CH, S5_STATE), S5_STATE ** -0.5),
        "od_c_im": nrm(ks[18], (N_ODD, S5_GROUPS, S5_GROUP_CH, S5_STATE), S5_STATE ** -0.5),
        "od_d": nrm(ks[19], (N_ODD, D_MODEL), 1.0),
        "od_w_glu": nrm(ks[20], (N_ODD, D_MODEL, 2 * D_MODEL), D_MODEL ** -0.5),
        "ffn_w_up": nrm(ks[21], (DEPTH, D_MODEL, 2 * D_FF), D_MODEL ** -0.5),
        "ffn_conv_w": nrm(ks[22], (DEPTH, CONV_W, 2 * D_FF), CONV_W ** -0.5),
        "ffn_conv_b": nrm(ks[23], (DEPTH, 2 * D_FF), 0.02),
        "ffn_w_down": nrm(ks[24], (DEPTH, D_FF, D_MODEL), D_FF ** -0.5),
        "ple_w_proj": nrm(ks[25], (DEPTH, PLE_DIM, D_MODEL), PLE_DIM ** -0.5),
        "ple_w_gate": nrm(ks[26], (DEPTH, D_MODEL, D_MODEL), D_MODEL ** -0.5),
    }


def reference(x, p, norm_mix, norm_ffn, ev_w_in, ev_b_fgate, ev_q_norm, ev_k_norm, ev_v_norm,
              ev_w_spatial, ev_b_spatial, ev_w_out, od_a_re, od_a_im, od_log_dt, od_b_re, od_b_im,
              od_c_re, od_c_im, od_d, od_w_glu, ffn_w_up, ffn_conv_w, ffn_conv_b, ffn_w_down,
              ple_w_proj, ple_w_gate):
    bsz, seq = x.shape[0], x.shape[1]
    splits = [A_WIDTH, 2 * A_WIDTH, 2 * A_WIDTH + B_WIDTH, 2 * A_WIDTH + 2 * B_WIDTH,
              2 * A_WIDTH + 3 * B_WIDTH]
    for i in range(DEPTH):
        h = rms_norm(x, norm_mix[i])
        if i % 2 == 0:
            e = i // 2
            z = h @ ev_w_in[e]
            u_a, v_a, q, k, v_b, f = jnp.split(z, splits, axis=-1)
            y_a = gmlp_mixer(u_a, v_a, ev_v_norm[e], ev_w_spatial[e], ev_b_spatial[e])
            shp = (bsz, seq, B_HEADS, B_HEAD_DIM)
            y_b = fox_attention(q.reshape(shp), k.reshape(shp), v_b.reshape(shp),
                                f + ev_b_fgate[e], ev_q_norm[e], ev_k_norm[e])
            x = x + jnp.concatenate([y_a, y_b], axis=-1) @ ev_w_out[e]
        else:
            o = i // 2
            y = s5_mixer(h, od_a_re[o], od_a_im[o], od_log_dt[o], od_b_re[o], od_b_im[o],
                         od_c_re[o], od_c_im[o], od_d[o])
            g_a, g_b = jnp.split(jax.nn.gelu(y) @ od_w_glu[o], 2, axis=-1)
            x = x + g_a * jax.nn.sigmoid(g_b)
        x = x + conv_ffn(rms_norm(x, norm_ffn[i]), ffn_w_up[i], ffn_conv_w[i], ffn_conv_b[i], ffn_w_down[i])
        gate = jax.nn.sigmoid(rms_norm(x) @ ple_w_gate[i])
        x = x + gate * (p[i] @ ple_w_proj[i])
    return x
```

```python
import functools
import math

import jax
import jax.numpy as jnp
from jax import lax
from jax.experimental import pallas as pl
from jax.experimental.pallas import tpu as pltpu

F32 = jnp.float32
BF16 = jnp.bfloat16
EPS = 1e-6
NEG_INF = -1e30
LANES = 128
SUBLANES = 8
VMEM_LIMIT = 56 * 1024 * 1024

A_GROUPS = 4
A_CHUNK = 128
N_HEADS = 8
HEAD_DIM = 64
S5_GROUP_CH = 16
S5_STATE = 64
CONV_W = 3

TM_PRE = 512
TQ = 512
TM_FFN = 512
FF_CHUNK = 256
T_S5 = 256
S5_PITCH = T_S5 + SUBLANES


def _const_spec(shape):
    n = len(shape)
    return pl.BlockSpec(shape, lambda *_: (0,) * n, pipeline_mode=pl.Buffered(1))


def _rms(x, n):
    return x * lax.rsqrt(jnp.sum(x * x, axis=-1, keepdims=True) * (1.0 / n) + EPS)


def _dot(a, b):
    return jnp.dot(a, b, preferred_element_type=F32)


def _split3(x):
    hi = x.astype(BF16)
    r = x - hi.astype(F32)
    mid = r.astype(BF16)
    lo = (r - mid.astype(F32)).astype(BF16)
    return hi, mid, lo


def _even_pre_kernel(x_ref, g_ref, wuv_ref, wq_ref, wk_ref, wv_ref, wf_ref, bf_ref,
                     qg_ref, kg_ref, qc_ref, vc_ref, vgain_ref, wsp_ref, bsp_ref, e_ref,
                     ya_ref, q_ref, k_ref, v_ref, carry_ref, *, tm, d_model, a_width):
    i = pl.program_id(0)

    @pl.when(i == 0)
    def _():
        carry_ref[...] = jnp.zeros_like(carry_ref)

    h = (_rms(x_ref[...], d_model) * g_ref[...]).astype(BF16)

    zuv = _dot(h, wuv_ref[...])
    u = jax.nn.gelu(zuv[:, :a_width])
    v = jax.nn.gelu(zuv[:, a_width:])
    tri = (lax.broadcasted_iota(jnp.int32, (A_CHUNK, A_CHUNK), 1)
           <= lax.broadcasted_iota(jnp.int32, (A_CHUNK, A_CHUNK), 0))
    for g in range(A_GROUPS):
        cols = slice(g * LANES, (g + 1) * LANES)
        vg = (_rms(v[:, cols], LANES) * vgain_ref[:, cols]).astype(BF16)
        w = jnp.where(tri, wsp_ref[g], 0.0).astype(BF16)
        for c in range(tm // A_CHUNK):
            rows = slice(c * A_CHUNK, (c + 1) * A_CHUNK)
            sv = _dot(w, vg[rows]) + bsp_ref[g]
            ya_ref[rows, cols] = (u[rows, cols] * sv).astype(BF16)

    f = _dot(h, wf_ref[...]) + bf_ref[...]
    ls = jnp.minimum(f, 0.0) - jnp.log1p(jnp.exp(-jnp.abs(f)))
    tri_t = (lax.broadcasted_iota(jnp.int32, (tm, tm), 1)
             <= lax.broadcasted_iota(jnp.int32, (tm, tm), 0)).astype(BF16)
    hi, mid, lo = _split3(ls)
    cum = _dot(tri_t, hi) + _dot(tri_t, mid) + _dot(tri_t, lo) + carry_ref[0:1, :]
    carry_ref[...] = jnp.broadcast_to(cum[tm - 1:tm, :], carry_ref.shape)
    chi, cmid, clo = _split3(cum)
    kadd = _dot(chi, e_ref[0]) + _dot(cmid, e_ref[1]) + _dot(clo, e_ref[2])

    zq = _dot(h, wq_ref[...])
    zk = _dot(h, wk_ref[...])
    zv = _dot(h, wv_ref[...])
    for hd in range(N_HEADS):
        cols = slice(hd * LANES, (hd + 1) * LANES)
        q_ref[hd] = (_rms(zq[:, cols], HEAD_DIM) * qg_ref[...] + qc_ref[...]).astype(BF16)
        k_ref[hd] = (_rms(zk[:, cols], HEAD_DIM) * kg_ref[...] + kadd[:, cols]).astype(BF16)
        v_ref[hd] = (zv[:, cols] + vc_ref[...]).astype(BF16)


def _pad_heads(w):
    d = w.shape[0]
    w = w.reshape(d, N_HEADS, HEAD_DIM)
    w = jnp.pad(w, ((0, 0), (0, 0), (0, LANES - HEAD_DIM)))
    return w.reshape(d, N_HEADS * LANES)


def _even_pre(x, g_mix, w_in, b_fgate, q_norm, k_norm, v_norm, w_sp, b_sp):
    s, d = x.shape
    a_width = A_GROUPS * LANES
    b_width = N_HEADS * HEAD_DIM
    tm = TM_PRE
    o = 2 * a_width
    wuv = w_in[:, :o].astype(BF16)
    wq = _pad_heads(w_in[:, o:o + b_width]).astype(BF16)
    wk = _pad_heads(w_in[:, o + b_width:o + 2 * b_width]).astype(BF16)
    wv = _pad_heads(w_in[:, o + 2 * b_width:o + 3 * b_width]).astype(BF16)
    wf = jnp.pad(w_in[:, o + 3 * b_width:], ((0, 0), (0, LANES - N_HEADS))).astype(BF16)
    bf = jnp.pad(b_fgate, (0, LANES - N_HEADS)).reshape(1, LANES)
    pad = LANES - HEAD_DIM
    qg = jnp.pad(q_norm * (HEAD_DIM ** -0.5), (0, pad)).reshape(1, LANES)
    kg = jnp.pad(k_norm, (0, pad)).reshape(1, LANES)
    lane = jnp.arange(LANES)
    qc = jnp.where((lane >= HEAD_DIM) & (lane < HEAD_DIM + 3), -1.0, 0.0).astype(F32).reshape(1, LANES)
    vc = jnp.where(lane == HEAD_DIM, 1.0, 0.0).astype(F32).reshape(1, LANES)
    hh = jnp.arange(N_HEADS)
    e = jnp.zeros((3, LANES, N_HEADS * LANES), F32)
    for part in range(3):
        e = e.at[part, hh, hh * LANES + HEAD_DIM + part].set(1.0)
    e = e.astype(BF16)
    bsp = jnp.broadcast_to(b_sp[:, :, None], (A_GROUPS, A_CHUNK, LANES))

    row = lambda i: (i, 0)
    hrow = lambda i: (0, i, 0)
    kern = functools.partial(_even_pre_kernel, tm=tm, d_model=d, a_width=a_width)
    hshape = jax.ShapeDtypeStruct((N_HEADS, s, LANES), BF16)
    return pl.pallas_call(
        kern,
        grid=(s // tm,),
        in_specs=[
            pl.BlockSpec((tm, d), row),
            _const_spec((1, d)),
            _const_spec(wuv.shape), _const_spec(wq.shape), _const_spec(wk.shape),
            _const_spec(wv.shape), _const_spec(wf.shape), _const_spec(bf.shape),
            _const_spec(qg.shape), _const_spec(kg.shape), _const_spec(qc.shape),
            _const_spec(vc.shape), _const_spec((1, a_width)), _const_spec(w_sp.shape),
            _const_spec(bsp.shape), _const_spec(e.shape),
        ],
        out_specs=[
            pl.BlockSpec((tm, a_width), row),
            pl.BlockSpec((N_HEADS, tm, LANES), hrow),
            pl.BlockSpec((N_HEADS, tm, LANES), hrow),
            pl.BlockSpec((N_HEADS, tm, LANES), hrow),
        ],
        out_shape=[jax.ShapeDtypeStruct((s, a_width), BF16), hshape, hshape, hshape],
        scratch_shapes=[pltpu.VMEM((SUBLANES, LANES), F32)],
        compiler_params=pltpu.CompilerParams(
            dimension_semantics=("arbitrary",), vmem_limit_bytes=VMEM_LIMIT),
        name="even_pre",
    )(x, g_mix.reshape(1, d), wuv, wq, wk, wv, wf, bf, qg, kg, qc, vc,
      v_norm.reshape(1, a_width), w_sp, bsp, e)


def _attention_kernel(q_ref, k_ref, v_ref, o_ref, *, tq):
    qi = pl.program_id(1)
    row = lax.broadcasted_iota(jnp.int32, (tq, tq), 0)
    col = lax.broadcasted_iota(jnp.int32, (tq, tq), 1)
    outs = []
    for hh in range(2):
        q = q_ref[hh]

        def step(kv, carry, masked, hh=hh, q=q):
            m, acc = carry
            start = pl.multiple_of(kv * tq, tq)
            kc = k_ref[hh, pl.ds(start, tq), :]
            vc = v_ref[hh, pl.ds(start, tq), :]
            s = lax.dot_general(q, kc, (((1,), (1,)), ((), ())), preferred_element_type=F32)
            if masked:
                s = jnp.where(col <= row, s, NEG_INF)
            m_new = jnp.maximum(m, jnp.max(s, axis=-1, keepdims=True))
            alpha = jnp.exp(m - m_new)
            p = jnp.exp(s - m_new).astype(BF16)
            acc = acc * alpha + _dot(p, vc)
            return m_new, acc

        init = (jnp.full((tq, 1), NEG_INF, F32), jnp.zeros((tq, LANES), F32))
        carry = lax.fori_loop(0, qi, functools.partial(step, masked=False), init)
        _, acc = step(qi, carry, masked=True)
        outs.append(acc[:, :HEAD_DIM] / acc[:, HEAD_DIM:HEAD_DIM + 1])
    o_ref[...] = jnp.concatenate(outs, axis=-1).astype(o_ref.dtype)


def _attention(q, k, v):
    _, s, _ = q.shape
    tq = TQ
    kern = functools.partial(_attention_kernel, tq=tq)
    return pl.pallas_call(
        kern,
        grid=(N_HEADS // 2, s // tq),
        in_specs=[
            pl.BlockSpec((2, tq, LANES), lambda hp, qi: (hp, qi, 0)),
            pl.BlockSpec((2, s, LANES), lambda hp, qi: (hp, 0, 0)),
            pl.BlockSpec((2, s, LANES), lambda hp, qi: (hp, 0, 0)),
        ],
        out_specs=pl.BlockSpec((tq, 2 * HEAD_DIM), lambda hp, qi: (qi, hp)),
        out_shape=jax.ShapeDtypeStruct((s, N_HEADS * HEAD_DIM), BF16),
        compiler_params=pltpu.CompilerParams(
            dimension_semantics=("arbitrary", "arbitrary"), vmem_limit_bytes=VMEM_LIMIT),
        name="attention",
    )(q, k, v)


def _ffn_ple_kernel(*refs, tm, d_model, n_chunks, with_mix):
    if with_mix:
        (x_ref, ya_ref, yb_ref, woa_ref, wob_ref, g_ref, wup_ref, cv_ref, wdn_ref, p_ref,
         wproj_ref, wgate_ref, o_ref, halo_ref, acc_ref, xn_ref) = refs
    else:
        (x_ref, g_ref, wup_ref, cv_ref, wdn_ref, p_ref,
         wproj_ref, wgate_ref, o_ref, halo_ref, acc_ref, xn_ref) = refs
    i = pl.program_id(0)

    @pl.when(i == 0)
    def _():
        halo_ref[...] = jnp.zeros_like(halo_ref)

    x1 = x_ref[...]
    if with_mix:
        x1 = x1 + _dot(ya_ref[...], woa_ref[...]) + _dot(yb_ref[...], wob_ref[...])
    xn_ref[...] = (_rms(x1, d_model) * g_ref[...]).astype(BF16)
    acc_ref[...] = x1

    fc = FF_CHUNK
    rows8 = lax.broadcasted_iota(jnp.int32, (SUBLANES, 2 * fc), 0)

    def conv(h, h1, h2, cv):
        return cv[3:4] + cv[0:1] * h2 + cv[1:2] * h1 + cv[2:3] * h

    def chunk(c, _):
        h = _dot(xn_ref[...], wup_ref[c])
        cv = cv_ref[c]
        prev = halo_ref[c]
        halo_ref[c] = h[tm - SUBLANES:, :]
        hc = conv(h, pltpu.roll(h, 1, 0), pltpu.roll(h, 2, 0), cv)
        h8 = h[:SUBLANES]
        h1 = jnp.where(rows8 < 1, pltpu.roll(prev, 1, 0), pltpu.roll(h8, 1, 0))
        h2 = jnp.where(rows8 < 2, pltpu.roll(prev, 2, 0), pltpu.roll(h8, 2, 0))
        hc = jnp.concatenate([conv(h8, h1, h2, cv), hc[SUBLANES:]], axis=0)
        gate, up = hc[:, :fc], hc[:, fc:]
        act = (gate * jax.nn.sigmoid(gate) * up).astype(BF16)
        acc_ref[...] += _dot(act, wdn_ref[c])
        return 0

    lax.fori_loop(0, n_chunks, chunk, 0)

    x2 = acc_ref[...]
    gate = jax.nn.sigmoid(_dot(_rms(x2, d_model).astype(BF16), wgate_ref[...]))
    o_ref[...] = x2 + gate * _dot(p_ref[...].astype(BF16), wproj_ref[...])


def _ffn_ple(x, g_ffn, w_up, conv_w, conv_b, w_down, p, w_proj, w_gate, mix=None):
    s, d = x.shape
    d_ff = w_down.shape[0]
    fc = FF_CHUNK
    nc = d_ff // fc
    tm = TM_FFN
    wg = w_up[:, :d_ff].reshape(d, nc, fc)
    wu = w_up[:, d_ff:].reshape(d, nc, fc)
    wup = jnp.concatenate([wg, wu], axis=-1).transpose(1, 0, 2).astype(BF16)
    cv = jnp.concatenate([conv_w, conv_b[None]], axis=0)
    cv = jnp.concatenate([cv[:, :d_ff].reshape(4, nc, fc), cv[:, d_ff:].reshape(4, nc, fc)], axis=-1)
    cv = jnp.pad(cv.transpose(1, 0, 2), ((0, 0), (0, SUBLANES - 4), (0, 0)))
    wdn = w_down.reshape(nc, fc, d).astype(BF16)
    pdim = p.shape[-1]

    row = lambda i: (i, 0)
    args, specs = [x], [pl.BlockSpec((tm, d), row)]
    if mix is not None:
        ya, yb, w_out = mix
        wa = w_out[:ya.shape[1]].astype(BF16)
        wb = w_out[ya.shape[1]:].astype(BF16)
        args += [ya, yb, wa, wb]
        specs += [pl.BlockSpec((tm, ya.shape[1]), row), pl.BlockSpec((tm, yb.shape[1]), row),
                  _const_spec(wa.shape), _const_spec(wb.shape)]
    args += [g_ffn.reshape(1, d), wup, cv, wdn, p, w_proj.astype(BF16), w_gate.astype(BF16)]
    specs += [_const_spec((1, d)), _const_spec(wup.shape), _const_spec(cv.shape),
              _const_spec(wdn.shape), pl.BlockSpec((tm, pdim), row),
              _const_spec(w_proj.shape), _const_spec(w_gate.shape)]
    kern = functools.partial(_ffn_ple_kernel, tm=tm, d_model=d, n_chunks=nc,
                             with_mix=mix is not None)
    return pl.pallas_call(
        kern,
        grid=(s // tm,),
        in_specs=specs,
        out_specs=pl.BlockSpec((tm, d), row),
        out_shape=jax.ShapeDtypeStruct((s, d), F32),
        scratch_shapes=[pltpu.VMEM((nc, SUBLANES, 2 * fc), F32),
                        pltpu.VMEM((tm, d), F32),
                        pltpu.VMEM((tm, d), BF16)],
        compiler_params=pltpu.CompilerParams(
            dimension_semantics=("arbitrary",), vmem_limit_bytes=VMEM_LIMIT),
        name="ffn_ple_mix" if mix is not None else "ffn_ple",
    )(*args)


def _s5_prep_kernel(are_ref, aim_ref, ldt_ref, bre_ref, bim_ref,
                    abre_ref, abim_ref, bbre_ref, bbim_ref):
    lr, li = are_ref[...], aim_ref[...]
    dt = jnp.exp(ldt_ref[...])
    mag = jnp.exp(lr * dt)
    ab_re = mag * jnp.cos(li * dt)
    ab_im = mag * jnp.sin(li * dt)
    den = lr * lr + li * li
    nr, ni = ab_re - 1.0, ab_im
    cr = (nr * lr + ni * li) / den
    ci = (ni * lr - nr * li) / den
    br, bi = bre_ref[...], bim_ref[...]
    abre_ref[...] = ab_re
    abim_ref[...] = ab_im
    bbre_ref[...] = cr * br - ci * bi
    bbim_ref[...] = cr * bi + ci * br


def _s5_prep(a_re, a_im, log_dt, b_re, b_im):
    g, pst, c = b_re.shape
    shape2 = (g * pst * c // LANES, LANES)
    rep = lambda a: jnp.broadcast_to(a[:, :, None], (g, pst, c)).reshape(shape2)
    ldt = jnp.broadcast_to(log_dt[:, None, None], (g, pst, c)).reshape(shape2)
    out = jax.ShapeDtypeStruct(shape2, F32)
    ab_re, ab_im, bb_re, bb_im = pl.pallas_call(
        _s5_prep_kernel, out_shape=[out] * 4, name="s5_prep",
    )(rep(a_re), rep(a_im), ldt, b_re.reshape(shape2), b_im.reshape(shape2))
    ab_re = ab_re.reshape(g, pst, c)[:, :, 0]
    ab_im = ab_im.reshape(g, pst, c)[:, :, 0]
    return ab_re, ab_im, bb_re.reshape(g, pst, c), bb_im.reshape(g, pst, c)


def _s5_kernel(x_ref, g_ref, wbre_ref, wbim_ref, a_ref, cre_ref, cim_ref, d_ref, wglu_ref,
               o_ref, bre_scr, bim_scr, st_scr, *, t_tile, pitch, d_model, n_slab, n_lb):
    i = pl.program_id(0)

    @pl.when(i == 0)
    def _():
        st_scr[...] = jnp.zeros_like(st_scr)

    xt = x_ref[...]
    h = _rms(xt, d_model) * g_ref[...]
    hb = h.astype(BF16)

    for j in range(n_slab):
        lhs = hb[:, j * LANES:(j + 1) * LANES]
        bre = _dot(lhs, wbre_ref[j])
        bim = _dot(lhs, wbim_ref[j])
        for k in range(n_lb):
            bre_scr[k, pl.ds(j * pitch, t_tile), :] = bre[:, k * LANES:(k + 1) * LANES]
            bim_scr[k, pl.ds(j * pitch, t_tile), :] = bim[:, k * LANES:(k + 1) * LANES]

    ar = [a_ref[0, :, k * LANES:(k + 1) * LANES] for k in range(n_lb)]
    ai = [a_ref[1, :, k * LANES:(k + 1) * LANES] for k in range(n_lb)]

    def body(t, carry):
        xr, xi = carry
        nxr, nxi = [], []
        for k in range(n_lb):
            idx = (k, pl.ds(t, SUBLANES, stride=pitch), slice(None))
            nr = ar[k] * xr[k] - ai[k] * xi[k] + bre_scr[idx]
            ni = ar[k] * xi[k] + ai[k] * xr[k] + bim_scr[idx]
            bre_scr[idx] = nr
            bim_scr[idx] = ni
            nxr.append(nr)
            nxi.append(ni)
        return tuple(nxr), tuple(nxi)

    init = (tuple(st_scr[0, k] for k in range(n_lb)), tuple(st_scr[1, k] for k in range(n_lb)))
    xr, xi = lax.fori_loop(0, t_tile, body, init)
    for k in range(n_lb):
        st_scr[0, k] = xr[k]
        st_scr[1, k] = xi[k]

    ys = []
    for j in range(n_slab):
        rows = pl.ds(j * pitch, t_tile)
        xre = jnp.concatenate([bre_scr[k, rows, :] for k in range(n_lb)], axis=-1).astype(BF16)
        xim = jnp.concatenate([bim_scr[k, rows, :] for k in range(n_lb)], axis=-1).astype(BF16)
        ys.append(_dot(xre, cre_ref[j]) - _dot(xim, cim_ref[j]))
    y = jnp.concatenate(ys, axis=-1) + d_ref[...] * h
    gl = _dot(jax.nn.gelu(y).astype(BF16), wglu_ref[...])
    o_ref[...] = xt + gl[:, :d_model] * jax.nn.sigmoid(gl[:, d_model:])


def _s5_layer(x, g_mix, a_re, a_im, log_dt, b_re, b_im, c_re, c_im, d_skip, w_glu):
    s, d = x.shape
    n_groups = d // S5_GROUP_CH
    gps = LANES // S5_GROUP_CH
    n_slab = d // LANES
    assert n_slab == SUBLANES
    slab_states = gps * S5_STATE
    n_lb = slab_states // LANES
    t_tile, pitch = T_S5, S5_PITCH

    ab_re, ab_im, bb_re, bb_im = _s5_prep(a_re, a_im, log_dt, b_re, b_im)
    eye = jnp.eye(gps, dtype=F32)

    def bmat(bb):
        bb = bb.reshape(n_slab, gps, S5_STATE, S5_GROUP_CH)
        w = jnp.einsum('jgpc,gh->jgchp', bb, eye)
        return w.reshape(n_slab, LANES, slab_states).astype(BF16)

    def cmat(cc):
        cc = cc.reshape(n_slab, gps, S5_GROUP_CH, S5_STATE)
        w = jnp.einsum('jgcp,gh->jgphc', cc, eye)
        return w.reshape(n_slab, slab_states, LANES).astype(BF16)

    a = jnp.stack([ab_re.reshape(n_slab, slab_states), ab_im.reshape(n_slab, slab_states)])
    wbre, wbim, cre, cim = bmat(bb_re), bmat(bb_im), cmat(c_re), cmat(c_im)
    wglu = w_glu.astype(BF16)

    row = lambda i: (i, 0)
    kern = functools.partial(_s5_kernel, t_tile=t_tile, pitch=pitch, d_model=d,
                             n_slab=n_slab, n_lb=n_lb)
    return pl.pallas_call(
        kern,
        grid=(s // t_tile,),
        in_specs=[
            pl.BlockSpec((t_tile, d), row),
            _const_spec((1, d)),
            _const_spec(wbre.shape), _const_spec(wbim.shape), _const_spec(a.shape),
            _const_spec(cre.shape), _const_spec(cim.shape), _const_spec((1, d)),
            _const_spec(wglu.shape),
        ],
        out_specs=pl.BlockSpec((t_tile, d), row),
        out_shape=jax.ShapeDtypeStruct((s, d), F32),
        scratch_shapes=[pltpu.VMEM((n_lb, n_slab * pitch, LANES), F32),
                        pltpu.VMEM((n_lb, n_slab * pitch, LANES), F32),
                        pltpu.VMEM((2, n_lb, SUBLANES, LANES), F32)],
        compiler_params=pltpu.CompilerParams(
            dimension_semantics=("arbitrary",), vmem_limit_bytes=VMEM_LIMIT),
        name="s5_layer",
    )(x, g_mix.reshape(1, d), wbre, wbim, a, cre, cim, d_skip.reshape(1, d), wglu)


def kernel(x, p, norm_mix, norm_ffn, ev_w_in, ev_b_fgate, ev_q_norm, ev_k_norm, ev_v_norm,
           ev_w_spatial, ev_b_spatial, ev_w_out, od_a_re, od_a_im, od_log_dt, od_b_re, od_b_im,
           od_c_re, od_c_im, od_d, od_w_glu, ffn_w_up, ffn_conv_w, ffn_conv_b, ffn_w_down,
           ple_w_proj, ple_w_gate):
    bsz, seq, d = x.shape
    depth = p.shape[0]
    outs = []
    for b in range(bsz):
        xb = x[b]
        for i in range(depth):
            mix = None
            if i % 2 == 0:
                e = i // 2
                ya, q, k, v = _even_pre(xb, norm_mix[i], ev_w_in[e], ev_b_fgate[e], ev_q_norm[e],
                                        ev_k_norm[e], ev_v_norm[e], ev_w_spatial[e], ev_b_spatial[e])
                yb = _attention(q, k, v)
                mix = (ya, yb, ev_w_out[e])
            else:
                o = i // 2
                xb = _s5_layer(xb, norm_mix[i], od_a_re[o], od_a_im[o], od_log_dt[o], od_b_re[o],
                               od_b_im[o], od_c_re[o], od_c_im[o], od_d[o], od_w_glu[o])
            xb = _ffn_ple(xb, norm_ffn[i], ffn_w_up[i], ffn_conv_w[i], ffn_conv_b[i],
                          ffn_w_down[i], p[i, b], ple_w_proj[i], ple_w_gate[i], mix=mix)
        outs.append(xb)
    return jnp.stack(outs)
```

```python
import functools
import math

import jax
import jax.numpy as jnp
from jax import lax
from jax.experimental import pallas as pl
from jax.experimental.pallas import tpu as pltpu

F32 = jnp.float32
BF16 = jnp.bfloat16
EPS = 1e-6
NEG_INF = -1e30
LANES = 128
SUBLANES = 8
VMEM_LIMIT = 56 * 1024 * 1024

A_GROUPS = 4
A_CHUNK = 128
N_HEADS = 8
HEAD_DIM = 64
S5_GROUP_CH = 16
S5_STATE = 64
CONV_W = 3

LOG2E = 1.4426950408889634
V_ROWS = 80

TILE_ATT = 512
SUB_K = 256
TM_FFN = 512
FF_CHUNK = 256
T_S5 = 256
S5_PITCH = T_S5 + SUBLANES


def _const_spec(shape):
    n = len(shape)
    return pl.BlockSpec(shape, lambda *_: (0,) * n, pipeline_mode=pl.Buffered(1))


def _rms(x, n):
    return x * lax.rsqrt(jnp.sum(x * x, axis=-1, keepdims=True) * (1.0 / n) + EPS)


def _dot(a, b):
    return jnp.dot(a, b, preferred_element_type=F32)


def _split3(x):
    hi = x.astype(BF16)
    r = x - hi.astype(F32)
    mid = r.astype(BF16)
    lo = (r - mid.astype(F32)).astype(BF16)
    return hi, mid, lo


def _even_pre_kernel(x_ref, g_ref, wuv_ref, wq_ref, wk_ref, wv_ref, wf_ref, bf_ref,
                     qg_ref, kg_ref, qc_ref, vc_ref, vgain_ref, wsp_ref, bsp_ref, e_ref,
                     ya_ref, qt_ref, k_ref, vt_ref, carry_ref, *, tm, d_model, a_width):
    i = pl.program_id(0)

    @pl.when(i == 0)
    def _():
        carry_ref[...] = jnp.zeros_like(carry_ref)

    h = (_rms(x_ref[...], d_model) * g_ref[...]).astype(BF16)

    zuv = _dot(h, wuv_ref[...])
    u = jax.nn.gelu(zuv[:, :a_width])
    v = jax.nn.gelu(zuv[:, a_width:])
    tri = (lax.broadcasted_iota(jnp.int32, (A_CHUNK, A_CHUNK), 1)
           <= lax.broadcasted_iota(jnp.int32, (A_CHUNK, A_CHUNK), 0))
    for g in range(A_GROUPS):
        cols = slice(g * LANES, (g + 1) * LANES)
        vg = (_rms(v[:, cols], LANES) * vgain_ref[:, cols]).astype(BF16)
        w = jnp.where(tri, wsp_ref[g], 0.0).astype(BF16)
        for c in range(tm // A_CHUNK):
            rows = slice(c * A_CHUNK, (c + 1) * A_CHUNK)
            sv = _dot(w, vg[rows]) + bsp_ref[g]
            ya_ref[rows, cols] = (u[rows, cols] * sv).astype(BF16)

    f = _dot(h, wf_ref[...]) + bf_ref[...]
    ls = jnp.minimum(f, 0.0) - jnp.log1p(jnp.exp(-jnp.abs(f)))
    tri_t = (lax.broadcasted_iota(jnp.int32, (tm, tm), 1)
             <= lax.broadcasted_iota(jnp.int32, (tm, tm), 0)).astype(BF16)
    hi, mid, lo = _split3(ls)
    cum = _dot(tri_t, hi) + _dot(tri_t, mid) + _dot(tri_t, lo) + carry_ref[0:1, :]
    carry_ref[...] = jnp.broadcast_to(cum[tm - 1:tm, :], carry_ref.shape)
    chi, cmid, clo = _split3(cum * LOG2E)
    kadd = _dot(chi, e_ref[0]) + _dot(cmid, e_ref[1]) + _dot(clo, e_ref[2])

    zq = _dot(h, wq_ref[...])
    zk = _dot(h, wk_ref[...])
    zv = _dot(h, wv_ref[...])
    for hd in range(N_HEADS):
        cols = slice(hd * LANES, (hd + 1) * LANES)
        qn = _rms(zq[:, cols], HEAD_DIM) * qg_ref[...] + qc_ref[...]
        qt_ref[hd, 0] = qn.T.astype(BF16)
        k_ref[hd] = (_rms(zk[:, cols], HEAD_DIM) * kg_ref[...] + kadd[:, cols]).astype(BF16)
        vt_ref[hd, 0] = (zv[:, cols] + vc_ref[...]).T[:V_ROWS].astype(BF16)


def _pad_heads(w):
    d = w.shape[0]
    w = w.reshape(d, N_HEADS, HEAD_DIM)
    w = jnp.pad(w, ((0, 0), (0, 0), (0, LANES - HEAD_DIM)))
    return w.reshape(d, N_HEADS * LANES)


def _even_pre(x, g_mix, w_in, b_fgate, q_norm, k_norm, v_norm, w_sp, b_sp):
    s, d = x.shape
    a_width = A_GROUPS * LANES
    b_width = N_HEADS * HEAD_DIM
    tm = TILE_ATT
    o = 2 * a_width
    wuv = w_in[:, :o].astype(BF16)
    wq = _pad_heads(w_in[:, o:o + b_width]).astype(BF16)
    wk = _pad_heads(w_in[:, o + b_width:o + 2 * b_width]).astype(BF16)
    wv = _pad_heads(w_in[:, o + 2 * b_width:o + 3 * b_width]).astype(BF16)
    wf = jnp.pad(w_in[:, o + 3 * b_width:], ((0, 0), (0, LANES - N_HEADS))).astype(BF16)
    bf = jnp.pad(b_fgate, (0, LANES - N_HEADS)).reshape(1, LANES)
    pad = LANES - HEAD_DIM
    qg = jnp.pad(q_norm * (HEAD_DIM ** -0.5 * LOG2E), (0, pad)).reshape(1, LANES)
    kg = jnp.pad(k_norm, (0, pad)).reshape(1, LANES)
    lane = jnp.arange(LANES)
    qc = jnp.where((lane >= HEAD_DIM) & (lane < HEAD_DIM + 3), -1.0, 0.0).astype(F32).reshape(1, LANES)
    vc = jnp.where(lane == HEAD_DIM, 1.0, 0.0).astype(F32).reshape(1, LANES)
    hh = jnp.arange(N_HEADS)
    e = jnp.zeros((3, LANES, N_HEADS * LANES), F32)
    for part in range(3):
        e = e.at[part, hh, hh * LANES + HEAD_DIM + part].set(1.0)
    e = e.astype(BF16)
    bsp = jnp.broadcast_to(b_sp[:, :, None], (A_GROUPS, A_CHUNK, LANES))

    row = lambda i: (i, 0)
    hrow = lambda i: (0, i, 0)
    trow = lambda i: (0, i, 0, 0)
    kern = functools.partial(_even_pre_kernel, tm=tm, d_model=d, a_width=a_width)
    return pl.pallas_call(
        kern,
        grid=(s // tm,),
        in_specs=[
            pl.BlockSpec((tm, d), row),
            _const_spec((1, d)),
            _const_spec(wuv.shape), _const_spec(wq.shape), _const_spec(wk.shape),
            _const_spec(wv.shape), _const_spec(wf.shape), _const_spec(bf.shape),
            _const_spec(qg.shape), _const_spec(kg.shape), _const_spec(qc.shape),
            _const_spec(vc.shape), _const_spec((1, a_width)), _const_spec(w_sp.shape),
            _const_spec(bsp.shape), _const_spec(e.shape),
        ],
        out_specs=[
            pl.BlockSpec((tm, a_width), row),
            pl.BlockSpec((N_HEADS, 1, LANES, tm), trow),
            pl.BlockSpec((N_HEADS, tm, LANES), hrow),
            pl.BlockSpec((N_HEADS, 1, V_ROWS, tm), trow),
        ],
        out_shape=[jax.ShapeDtypeStruct((s, a_width), BF16),
                   jax.ShapeDtypeStruct((N_HEADS, s // tm, LANES, tm), BF16),
                   jax.ShapeDtypeStruct((N_HEADS, s, LANES), BF16),
                   jax.ShapeDtypeStruct((N_HEADS, s // tm, V_ROWS, tm), BF16)],
        scratch_shapes=[pltpu.VMEM((SUBLANES, LANES), F32)],
        compiler_params=pltpu.CompilerParams(
            dimension_semantics=("arbitrary",), vmem_limit_bytes=VMEM_LIMIT),
        name="even_pre",
    )(x, g_mix.reshape(1, d), wuv, wq, wk, wv, wf, bf, qg, kg, qc, vc,
      v_norm.reshape(1, a_width), w_sp, bsp, e)


def _attention_kernel(qt_ref, k_ref, vt_ref, o_ref, s0_ref, s1_ref, *, tq):
    qi = pl.program_id(1)
    krow = lax.broadcasted_iota(jnp.int32, (SUB_K, tq), 0)
    qcol = lax.broadcasted_iota(jnp.int32, (SUB_K, tq), 1)

    n_sub = tq // SUB_K

    def scores(s_ref, kv):
        start = pl.multiple_of(kv * tq, tq)
        for sub in range(n_sub):
            for hh in range(2):
                kc = k_ref[hh, pl.ds(start + sub * SUB_K, SUB_K), :]
                s_ref[hh, sub * SUB_K:(sub + 1) * SUB_K, :] = _dot(kc, qt_ref[hh, 0])

    def softmax_pv(s_ref, kv, state, masked):
        state = list(state)
        for sub in range(n_sub):
            for hh in range(2):
                m, acc = state[hh]
                st = s_ref[hh, sub * SUB_K:(sub + 1) * SUB_K, :]
                if masked:
                    st = jnp.where(krow + sub * SUB_K <= qcol, st, NEG_INF)
                m_new = jnp.maximum(m, jnp.max(st, axis=0, keepdims=True))
                alpha = jnp.exp2(m - m_new)
                pt = jnp.exp2(st - m_new).astype(BF16)
                vt = vt_ref[hh, kv][:, sub * SUB_K:(sub + 1) * SUB_K]
                state[hh] = (m_new, acc * alpha + _dot(vt, pt))
        return tuple(state)

    def pair(jj, state):
        c0 = 2 * jj
        scores(s1_ref, c0 + 1)
        state = softmax_pv(s0_ref, c0, state, masked=False)
        scores(s0_ref, c0 + 2)
        return softmax_pv(s1_ref, c0 + 1, state, masked=False)

    def tail_odd(state):
        scores(s1_ref, qi)
        state = softmax_pv(s0_ref, qi - 1, state, masked=False)
        return softmax_pv(s1_ref, qi, state, masked=True)

    def tail_even(state):
        return softmax_pv(s0_ref, qi, state, masked=True)

    init = tuple((jnp.full((1, tq), NEG_INF, F32), jnp.zeros((V_ROWS, tq), F32)) for _ in range(2))
    scores(s0_ref, 0)
    state = lax.fori_loop(0, qi // 2, pair, init)
    state = lax.cond(qi % 2 == 1, tail_odd, tail_even, state)
    ot = jnp.concatenate([acc[:HEAD_DIM] / acc[HEAD_DIM:HEAD_DIM + 1] for _, acc in state], axis=0)
    o_ref[...] = ot.T.astype(o_ref.dtype)


def _attention(qt, k, vt):
    _, s, _ = k.shape
    tq = TILE_ATT
    nk = s // tq
    kern = functools.partial(_attention_kernel, tq=tq)
    return pl.pallas_call(
        kern,
        grid=(N_HEADS // 2, s // tq),
        in_specs=[
            pl.BlockSpec((2, 1, LANES, tq), lambda hp, qi: (hp, qi, 0, 0)),
            pl.BlockSpec((2, s, LANES), lambda hp, qi: (hp, 0, 0)),
            pl.BlockSpec((2, nk, V_ROWS, tq), lambda hp, qi: (hp, 0, 0, 0)),
        ],
        out_specs=pl.BlockSpec((tq, 2 * HEAD_DIM), lambda hp, qi: (qi, hp)),
        out_shape=jax.ShapeDtypeStruct((s, N_HEADS * HEAD_DIM), BF16),
        scratch_shapes=[pltpu.VMEM((2, tq, tq), F32), pltpu.VMEM((2, tq, tq), F32)],
        compiler_params=pltpu.CompilerParams(
            dimension_semantics=("arbitrary", "arbitrary"), vmem_limit_bytes=VMEM_LIMIT),
        name="attention",
    )(qt, k, vt)


def _ffn_ple_kernel(*refs, tm, d_model, n_chunks, with_mix):
    if with_mix:
        (x_ref, ya_ref, yb_ref, woa_ref, wob_ref, g_ref, wup_ref, cv_ref, wdn_ref, p_ref,
         wproj_ref, wgate_ref, o_ref, halo_ref, acc_ref, xn_ref, h0_ref, h1_ref) = refs
    else:
        (x_ref, g_ref, wup_ref, cv_ref, wdn_ref, p_ref,
         wproj_ref, wgate_ref, o_ref, halo_ref, acc_ref, xn_ref, h0_ref, h1_ref) = refs
    i = pl.program_id(0)

    @pl.when(i == 0)
    def _():
        halo_ref[...] = jnp.zeros_like(halo_ref)

    x1 = x_ref[...]
    if with_mix:
        x1 = x1 + _dot(ya_ref[...], woa_ref[...]) + _dot(yb_ref[...], wob_ref[...])
    xn_ref[...] = (_rms(x1, d_model) * g_ref[...]).astype(BF16)
    acc_ref[...] = x1

    fc = FF_CHUNK

    def up_proj(h_ref, c):
        h_ref[SUBLANES:, :] = _dot(xn_ref[...], wup_ref[c])

    def process(h_ref, c):
        h_ref[:SUBLANES, :] = halo_ref[c]
        halo_ref[c] = h_ref[tm:, :]
        cv = cv_ref[c]
        hc = (cv[3:4] + cv[0:1] * h_ref[SUBLANES - 2:tm + SUBLANES - 2, :]
              + cv[1:2] * h_ref[SUBLANES - 1:tm + SUBLANES - 1, :] + cv[2:3] * h_ref[SUBLANES:, :])
        half_gate, up = 0.5 * hc[:, :fc], hc[:, fc:]
        act = ((half_gate + half_gate * jnp.tanh(half_gate)) * up).astype(BF16)
        acc_ref[...] += _dot(act, wdn_ref[c])

    def pair(jj, carry):
        c0 = 2 * jj
        up_proj(h1_ref, c0 + 1)
        process(h0_ref, c0)
        up_proj(h0_ref, c0 + 2)
        process(h1_ref, c0 + 1)
        return carry

    n_pairs = (n_chunks - 1) // 2
    up_proj(h0_ref, 0)
    lax.fori_loop(0, n_pairs, pair, 0)
    if n_chunks - 2 * n_pairs == 2:
        up_proj(h1_ref, n_chunks - 1)
        process(h0_ref, n_chunks - 2)
        process(h1_ref, n_chunks - 1)
    else:
        process(h0_ref, n_chunks - 1)

    x2 = acc_ref[...]
    gate = jax.nn.sigmoid(_dot(_rms(x2, d_model).astype(BF16), wgate_ref[...]))
    o_ref[...] = x2 + gate * _dot(p_ref[...].astype(BF16), wproj_ref[...])


def _ffn_ple(x, g_ffn, w_up, conv_w, conv_b, w_down, p, w_proj, w_gate, mix=None):
    s, d = x.shape
    d_ff = w_down.shape[0]
    fc = FF_CHUNK
    nc = d_ff // fc
    tm = TM_FFN
    wg = w_up[:, :d_ff].reshape(d, nc, fc)
    wu = w_up[:, d_ff:].reshape(d, nc, fc)
    wup = jnp.concatenate([wg, wu], axis=-1).transpose(1, 0, 2).astype(BF16)
    cv = jnp.concatenate([conv_w, conv_b[None]], axis=0)
    cv = jnp.concatenate([cv[:, :d_ff].reshape(4, nc, fc), cv[:, d_ff:].reshape(4, nc, fc)], axis=-1)
    cv = jnp.pad(cv.transpose(1, 0, 2), ((0, 0), (0, SUBLANES - 4), (0, 0)))
    wdn = w_down.reshape(nc, fc, d).astype(BF16)
    pdim = p.shape[-1]

    row = lambda i: (i, 0)
    args, specs = [x], [pl.BlockSpec((tm, d), row)]
    if mix is not None:
        ya, yb, w_out = mix
        wa = w_out[:ya.shape[1]].astype(BF16)
        wb = w_out[ya.shape[1]:].astype(BF16)
        args += [ya, yb, wa, wb]
        specs += [pl.BlockSpec((tm, ya.shape[1]), row), pl.BlockSpec((tm, yb.shape[1]), row),
                  _const_spec(wa.shape), _const_spec(wb.shape)]
    args += [g_ffn.reshape(1, d), wup, cv, wdn, p, w_proj.astype(BF16), w_gate.astype(BF16)]
    specs += [_const_spec((1, d)), _const_spec(wup.shape), _const_spec(cv.shape),
              _const_spec(wdn.shape), pl.BlockSpec((tm, pdim), row),
              _const_spec(w_proj.shape), _const_spec(w_gate.shape)]
    kern = functools.partial(_ffn_ple_kernel, tm=tm, d_model=d, n_chunks=nc,
                             with_mix=mix is not None)
    return pl.pallas_call(
        kern,
        grid=(s // tm,),
        in_specs=specs,
        out_specs=pl.BlockSpec((tm, d), row),
        out_shape=jax.ShapeDtypeStruct((s, d), F32),
        scratch_shapes=[pltpu.VMEM((nc, SUBLANES, 2 * fc), F32),
                        pltpu.VMEM((tm, d), F32),
                        pltpu.VMEM((tm, d), BF16),
                        pltpu.VMEM((tm + SUBLANES, 2 * fc), F32),
                        pltpu.VMEM((tm + SUBLANES, 2 * fc), F32)],
        compiler_params=pltpu.CompilerParams(
            dimension_semantics=("arbitrary",), vmem_limit_bytes=VMEM_LIMIT),
        name="ffn_ple_mix" if mix is not None else "ffn_ple",
    )(*args)


def _s5_prep_kernel(are_ref, aim_ref, ldt_ref, bre_ref, bim_ref,
                    abre_ref, abim_ref, bbre_ref, bbim_ref):
    lr, li = are_ref[...], aim_ref[...]
    dt = jnp.exp(ldt_ref[...])
    mag = jnp.exp(lr * dt)
    ab_re = mag * jnp.cos(li * dt)
    ab_im = mag * jnp.sin(li * dt)
    den = lr * lr + li * li
    nr, ni = ab_re - 1.0, ab_im
    cr = (nr * lr + ni * li) / den
    ci = (ni * lr - nr * li) / den
    br, bi = bre_ref[...], bim_ref[...]
    abre_ref[...] = ab_re
    abim_ref[...] = ab_im
    bbre_ref[...] = cr * br - ci * bi
    bbim_ref[...] = cr * bi + ci * br


def _s5_prep(a_re, a_im, log_dt, b_re, b_im):
    g, pst, c = b_re.shape
    shape2 = (g * pst * c // LANES, LANES)
    rep = lambda a: jnp.broadcast_to(a[:, :, None], (g, pst, c)).reshape(shape2)
    ldt = jnp.broadcast_to(log_dt[:, None, None], (g, pst, c)).reshape(shape2)
    out = jax.ShapeDtypeStruct(shape2, F32)
    ab_re, ab_im, bb_re, bb_im = pl.pallas_call(
        _s5_prep_kernel, out_shape=[out] * 4, name="s5_prep",
    )(rep(a_re), rep(a_im), ldt, b_re.reshape(shape2), b_im.reshape(shape2))
    ab_re = ab_re.reshape(g, pst, c)[:, :, 0]
    ab_im = ab_im.reshape(g, pst, c)[:, :, 0]
    return ab_re, ab_im, bb_re.reshape(g, pst, c), bb_im.reshape(g, pst, c)


def _s5_kernel(x_ref, g_ref, wbre_ref, wbim_ref, a_ref, cre_ref, cim_ref, d_ref, wglu_ref,
               o_ref, bre_scr, bim_scr, st_scr, *, t_tile, pitch, d_model, n_slab, n_lb):
    i = pl.program_id(0)

    @pl.when(i == 0)
    def _():
        st_scr[...] = jnp.zeros_like(st_scr)

    xt = x_ref[...]
    h = _rms(xt, d_model) * g_ref[...]
    hb = h.astype(BF16)

    for j in range(n_slab):
        lhs = hb[:, j * LANES:(j + 1) * LANES]
        bre = _dot(lhs, wbre_ref[j])
        bim = _dot(lhs, wbim_ref[j])
        for k in range(n_lb):
            bre_scr[k, pl.ds(j * pitch, t_tile), :] = bre[:, k * LANES:(k + 1) * LANES]
            bim_scr[k, pl.ds(j * pitch, t_tile), :] = bim[:, k * LANES:(k + 1) * LANES]

    ar = [a_ref[0, :, k * LANES:(k + 1) * LANES] for k in range(n_lb)]
    ai = [a_ref[1, :, k * LANES:(k + 1) * LANES] for k in range(n_lb)]

    def body(t, carry):
        xr, xi = carry
        nxr, nxi = [], []
        for k in range(n_lb):
            idx = (k, pl.ds(t, SUBLANES, stride=pitch), slice(None))
            nr = ar[k] * xr[k] - ai[k] * xi[k] + bre_scr[idx]
            ni = ar[k] * xi[k] + ai[k] * xr[k] + bim_scr[idx]
            bre_scr[idx] = nr
            bim_scr[idx] = ni
            nxr.append(nr)
            nxi.append(ni)
        return tuple(nxr), tuple(nxi)

    init = (tuple(st_scr[0, k] for k in range(n_lb)), tuple(st_scr[1, k] for k in range(n_lb)))
    xr, xi = lax.fori_loop(0, t_tile, body, init)
    for k in range(n_lb):
        st_scr[0, k] = xr[k]
        st_scr[1, k] = xi[k]

    ys = []
    for j in range(n_slab):
        rows = pl.ds(j * pitch, t_tile)
        xre = jnp.concatenate([bre_scr[k, rows, :] for k in range(n_lb)], axis=-1).astype(BF16)
        xim = jnp.concatenate([bim_scr[k, rows, :] for k in range(n_lb)], axis=-1).astype(BF16)
        ys.append(_dot(xre, cre_ref[j]) - _dot(xim, cim_ref[j]))
    y = jnp.concatenate(ys, axis=-1) + d_ref[...] * h
    gl = _dot(jax.nn.gelu(y).astype(BF16), wglu_ref[...])
    o_ref[...] = xt + gl[:, :d_model] * jax.nn.sigmoid(gl[:, d_model:])


def _s5_layer(x, g_mix, a_re, a_im, log_dt, b_re, b_im, c_re, c_im, d_skip, w_glu):
    s, d = x.shape
    n_groups = d // S5_GROUP_CH
    gps = LANES // S5_GROUP_CH
    n_slab = d // LANES
    assert n_slab == SUBLANES
    slab_states = gps * S5_STATE
    n_lb = slab_states // LANES
    t_tile, pitch = T_S5, S5_PITCH

    ab_re, ab_im, bb_re, bb_im = _s5_prep(a_re, a_im, log_dt, b_re, b_im)
    eye = jnp.eye(gps, dtype=F32)

    def bmat(bb):
        bb = bb.reshape(n_slab, gps, S5_STATE, S5_GROUP_CH)
        w = jnp.einsum('jgpc,gh->jgchp', bb, eye)
        return w.reshape(n_slab, LANES, slab_states).astype(BF16)

    def cmat(cc):
        cc = cc.reshape(n_slab, gps, S5_GROUP_CH, S5_STATE)
        w = jnp.einsum('jgcp,gh->jgphc', cc, eye)
        return w.reshape(n_slab, slab_states, LANES).astype(BF16)

    a = jnp.stack([ab_re.reshape(n_slab, slab_states), ab_im.reshape(n_slab, slab_states)])
    wbre, wbim, cre, cim = bmat(bb_re), bmat(bb_im), cmat(c_re), cmat(c_im)
    wglu = w_glu.astype(BF16)

    row = lambda i: (i, 0)
    kern = functools.partial(_s5_kernel, t_tile=t_tile, pitch=pitch, d_model=d,
                             n_slab=n_slab, n_lb=n_lb)
    return pl.pallas_call(
        kern,
        grid=(s // t_tile,),
        in_specs=[
            pl.BlockSpec((t_tile, d), row),
            _const_spec((1, d)),
            _const_spec(wbre.shape), _const_spec(wbim.shape), _const_spec(a.shape),
            _const_spec(cre.shape), _const_spec(cim.shape), _const_spec((1, d)),
            _const_spec(wglu.shape),
        ],
        out_specs=pl.BlockSpec((t_tile, d), row),
        out_shape=jax.ShapeDtypeStruct((s, d), F32),
        scratch_shapes=[pltpu.VMEM((n_lb, n_slab * pitch, LANES), F32),
                        pltpu.VMEM((n_lb, n_slab * pitch, LANES), F32),
                        pltpu.VMEM((2, n_lb, SUBLANES, LANES), F32)],
        compiler_params=pltpu.CompilerParams(
            dimension_semantics=("arbitrary",), vmem_limit_bytes=VMEM_LIMIT),
        name="s5_layer",
    )(x, g_mix.reshape(1, d), wbre, wbim, a, cre, cim, d_skip.reshape(1, d), wglu)


def kernel(x, p, norm_mix, norm_ffn, ev_w_in, ev_b_fgate, ev_q_norm, ev_k_norm, ev_v_norm,
           ev_w_spatial, ev_b_spatial, ev_w_out, od_a_re, od_a_im, od_log_dt, od_b_re, od_b_im,
           od_c_re, od_c_im, od_d, od_w_glu, ffn_w_up, ffn_conv_w, ffn_conv_b, ffn_w_down,
           ple_w_proj, ple_w_gate):
    bsz, seq, d = x.shape
    depth = p.shape[0]
    outs = []
    for b in range(bsz):
        xb = x[b]
        for i in range(depth):
            mix = None
            if i % 2 == 0:
                e = i // 2
                ya, qt, k, vt = _even_pre(xb, norm_mix[i], ev_w_in[e], ev_b_fgate[e], ev_q_norm[e],
                                          ev_k_norm[e], ev_v_norm[e], ev_w_spatial[e], ev_b_spatial[e])
                yb = _attention(qt, k, vt)
                mix = (ya, yb, ev_w_out[e])
            else:
                o = i // 2
                xb = _s5_layer(xb, norm_mix[i], od_a_re[o], od_a_im[o], od_log_dt[o], od_b_re[o],
                               od_b_im[o], od_c_re[o], od_c_im[o], od_d[o], od_w_glu[o])
            xb = _ffn_ple(xb, norm_ffn[i], ffn_w_up[i], ffn_conv_w[i], ffn_conv_b[i],
                          ffn_w_down[i], p[i, b], ple_w_proj[i], ple_w_gate[i], mix=mix)
        outs.append(xb)
    return jnp.stack(outs)
```

```python
import functools
import math

import jax
import jax.numpy as jnp
from jax import lax
from jax.experimental import pallas as pl
from jax.experimental.pallas import tpu as pltpu

F32 = jnp.float32
BF16 = jnp.bfloat16
EPS = 1e-6
NEG_INF = -1e30
LANES = 128
SUBLANES = 8
VMEM_LIMIT = 56 * 1024 * 1024

A_GROUPS = 4
A_CHUNK = 128
N_HEADS = 8
HEAD_DIM = 64
S5_GROUP_CH = 16
S5_STATE = 64
CONV_W = 3

LOG2E = 1.4426950408889634
V_ROWS = 80

TILE_ATT = 512
TQ_ATT = 1024
SUB_K = 256
TM_FFN = 512
FF_CHUNK = 256
T_S5 = 256
S5_PITCH = T_S5 + 4
SCAN_UNROLL = 4


def _const_spec(shape):
    n = len(shape)
    return pl.BlockSpec(shape, lambda *_: (0,) * n, pipeline_mode=pl.Buffered(1))


def _rms(x, n):
    return x * lax.rsqrt(jnp.sum(x * x, axis=-1, keepdims=True) * (1.0 / n) + EPS)


def _dot(a, b):
    return jnp.dot(a, b, preferred_element_type=F32)


def _split3(x):
    hi = x.astype(BF16)
    r = x - hi.astype(F32)
    mid = r.astype(BF16)
    lo = (r - mid.astype(F32)).astype(BF16)
    return hi, mid, lo


def _even_pre_kernel(x_ref, g_ref, wuv_ref, wq_ref, wk_ref, wv_ref, wf_ref, bf_ref,
                     qg_ref, kg_ref, qc_ref, vc_ref, vgain_ref, wsp_ref, bsp_ref, e_ref,
                     ya_ref, qt_ref, k_ref, vt_ref, carry_ref, *, tm, d_model, a_width):
    i = pl.program_id(0)

    @pl.when(i == 0)
    def _():
        carry_ref[...] = jnp.zeros_like(carry_ref)

    h = (_rms(x_ref[...], d_model) * g_ref[...]).astype(BF16)

    zuv = _dot(h, wuv_ref[...])
    u = jax.nn.gelu(zuv[:, :a_width])
    v = jax.nn.gelu(zuv[:, a_width:])
    tri = (lax.broadcasted_iota(jnp.int32, (A_CHUNK, A_CHUNK), 1)
           <= lax.broadcasted_iota(jnp.int32, (A_CHUNK, A_CHUNK), 0))
    for g in range(A_GROUPS):
        cols = slice(g * LANES, (g + 1) * LANES)
        vg = (_rms(v[:, cols], LANES) * vgain_ref[:, cols]).astype(BF16)
        w = jnp.where(tri, wsp_ref[g], 0.0).astype(BF16)
        for c in range(tm // A_CHUNK):
            rows = slice(c * A_CHUNK, (c + 1) * A_CHUNK)
            sv = _dot(w, vg[rows]) + bsp_ref[g]
            ya_ref[rows, cols] = (u[rows, cols] * sv).astype(BF16)

    f = _dot(h, wf_ref[...]) + bf_ref[...]
    ls = jnp.minimum(f, 0.0) - jnp.log1p(jnp.exp(-jnp.abs(f)))
    tri_t = (lax.broadcasted_iota(jnp.int32, (tm, tm), 1)
             <= lax.broadcasted_iota(jnp.int32, (tm, tm), 0)).astype(BF16)
    hi, mid, lo = _split3(ls)
    cum = _dot(tri_t, hi) + _dot(tri_t, mid) + _dot(tri_t, lo) + carry_ref[0:1, :]
    carry_ref[...] = jnp.broadcast_to(cum[tm - 1:tm, :], carry_ref.shape)
    chi, cmid, clo = _split3(cum * LOG2E)
    kadd = _dot(chi, e_ref[0]) + _dot(cmid, e_ref[1]) + _dot(clo, e_ref[2])

    zq = _dot(h, wq_ref[...])
    zk = _dot(h, wk_ref[...])
    zv = _dot(h, wv_ref[...])
    for hd in range(N_HEADS):
        cols = slice(hd * LANES, (hd + 1) * LANES)
        qn = _rms(zq[:, cols], HEAD_DIM) * qg_ref[...] + qc_ref[...]
        qt_ref[hd, 0] = qn.T.astype(BF16)
        k_ref[hd] = (_rms(zk[:, cols], HEAD_DIM) * kg_ref[...] + kadd[:, cols]).astype(BF16)
        vt_ref[hd, 0] = (zv[:, cols] + vc_ref[...]).T[:V_ROWS].astype(BF16)


def _pad_heads(w):
    d = w.shape[0]
    w = w.reshape(d, N_HEADS, HEAD_DIM)
    w = jnp.pad(w, ((0, 0), (0, 0), (0, LANES - HEAD_DIM)))
    return w.reshape(d, N_HEADS * LANES)


def _even_pre(x, g_mix, w_in, b_fgate, q_norm, k_norm, v_norm, w_sp, b_sp):
    s, d = x.shape
    a_width = A_GROUPS * LANES
    b_width = N_HEADS * HEAD_DIM
    tm = TILE_ATT
    o = 2 * a_width
    wuv = w_in[:, :o].astype(BF16)
    wq = _pad_heads(w_in[:, o:o + b_width]).astype(BF16)
    wk = _pad_heads(w_in[:, o + b_width:o + 2 * b_width]).astype(BF16)
    wv = _pad_heads(w_in[:, o + 2 * b_width:o + 3 * b_width]).astype(BF16)
    wf = jnp.pad(w_in[:, o + 3 * b_width:], ((0, 0), (0, LANES - N_HEADS))).astype(BF16)
    bf = jnp.pad(b_fgate, (0, LANES - N_HEADS)).reshape(1, LANES)
    pad = LANES - HEAD_DIM
    qg = jnp.pad(q_norm * (HEAD_DIM ** -0.5 * LOG2E), (0, pad)).reshape(1, LANES)
    kg = jnp.pad(k_norm, (0, pad)).reshape(1, LANES)
    lane = jnp.arange(LANES)
    qc = jnp.where((lane >= HEAD_DIM) & (lane < HEAD_DIM + 3), -1.0, 0.0).astype(F32).reshape(1, LANES)
    vc = jnp.where(lane == HEAD_DIM, 1.0, 0.0).astype(F32).reshape(1, LANES)
    hh = jnp.arange(N_HEADS)
    e = jnp.zeros((3, LANES, N_HEADS * LANES), F32)
    for part in range(3):
        e = e.at[part, hh, hh * LANES + HEAD_DIM + part].set(1.0)
    e = e.astype(BF16)
    bsp = jnp.broadcast_to(b_sp[:, :, None], (A_GROUPS, A_CHUNK, LANES))

    row = lambda i: (i, 0)
    hrow = lambda i: (0, i, 0)
    trow = lambda i: (0, i, 0, 0)
    qsplit = TQ_ATT // tm
    kern = functools.partial(_even_pre_kernel, tm=tm, d_model=d, a_width=a_width)
    return pl.pallas_call(
        kern,
        grid=(s // tm,),
        in_specs=[
            pl.BlockSpec((tm, d), row),
            _const_spec((1, d)),
            _const_spec(wuv.shape), _const_spec(wq.shape), _const_spec(wk.shape),
            _const_spec(wv.shape), _const_spec(wf.shape), _const_spec(bf.shape),
            _const_spec(qg.shape), _const_spec(kg.shape), _const_spec(qc.shape),
            _const_spec(vc.shape), _const_spec((1, a_width)), _const_spec(w_sp.shape),
            _const_spec(bsp.shape), _const_spec(e.shape),
        ],
        out_specs=[
            pl.BlockSpec((tm, a_width), row),
            pl.BlockSpec((N_HEADS, 1, LANES, tm), lambda i: (0, i // qsplit, 0, i % qsplit)),
            pl.BlockSpec((N_HEADS, tm, LANES), hrow),
            pl.BlockSpec((N_HEADS, 1, V_ROWS, tm), trow),
        ],
        out_shape=[jax.ShapeDtypeStruct((s, a_width), BF16),
                   jax.ShapeDtypeStruct((N_HEADS, s // TQ_ATT, LANES, TQ_ATT), BF16),
                   jax.ShapeDtypeStruct((N_HEADS, s, LANES), BF16),
                   jax.ShapeDtypeStruct((N_HEADS, s // tm, V_ROWS, tm), BF16)],
        scratch_shapes=[pltpu.VMEM((SUBLANES, LANES), F32)],
        compiler_params=pltpu.CompilerParams(
            dimension_semantics=("arbitrary",), vmem_limit_bytes=VMEM_LIMIT),
        name="even_pre",
    )(x, g_mix.reshape(1, d), wuv, wq, wk, wv, wf, bf, qg, kg, qc, vc,
      v_norm.reshape(1, a_width), w_sp, bsp, e)


def _attention_kernel(qt_ref, k_ref, vt_ref, o_ref, s0_ref, s1_ref, mx0_ref, mx1_ref, *, tq, tk):
    qi = pl.program_id(1)
    krow = lax.broadcasted_iota(jnp.int32, (SUB_K, tq), 0)
    qcol = lax.broadcasted_iota(jnp.int32, (SUB_K, tq), 1)
    n_sub = tk // SUB_K

    pieces = [(sub, hh) for sub in range(n_sub) for hh in range(2)]

    def scores(buf, kv, sub, hh):
        s_ref, mx_ref = buf
        start = pl.multiple_of(kv * tk, tk)
        kc = k_ref[hh, pl.ds(start + sub * SUB_K, SUB_K), :]
        st = _dot(kc, qt_ref[hh, 0])
        s_ref[hh, sub * SUB_K:(sub + 1) * SUB_K, :] = st
        mx_ref[sub * 2 + hh] = jnp.max(st, axis=0, keepdims=True)

    def softmax_pv(buf, kv, state, diag, sub, hh):
        s_ref, mx_ref = buf
        m, acc = state[hh]
        st = s_ref[hh, sub * SUB_K:(sub + 1) * SUB_K, :]
        if diag is None:
            mx = mx_ref[sub * 2 + hh]
        else:
            st = jnp.where(krow + (diag * tk + sub * SUB_K) <= qcol, st, NEG_INF)
            mx = jnp.max(st, axis=0, keepdims=True)
        m_new = jnp.maximum(m, mx)
        alpha = jnp.exp2(m - m_new)
        pt = jnp.exp2(st - m_new).astype(BF16)
        vt = vt_ref[hh, kv][:, sub * SUB_K:(sub + 1) * SUB_K]
        state[hh] = (m_new, acc * alpha + _dot(vt, pt))

    def stage(state, cur, cur_kv, diag, nxt=None, nxt_kv=None):
        state = list(state)
        for sub, hh in pieces:
            if nxt is not None:
                scores(nxt, nxt_kv, sub, hh)
            softmax_pv(cur, cur_kv, state, diag, sub, hh)
        return tuple(state)

    buf0, buf1 = (s0_ref, mx0_ref), (s1_ref, mx1_ref)

    def pair(jj, state):
        c0 = 2 * jj
        state = stage(state, buf0, c0, None, buf1, c0 + 1)
        return stage(state, buf1, c0 + 1, None, buf0, c0 + 2)

    init = tuple((jnp.full((1, tq), NEG_INF, F32), jnp.zeros((V_ROWS, tq), F32)) for _ in range(2))
    for sub, hh in pieces:
        scores(buf0, 0, sub, hh)
    state = lax.fori_loop(0, qi, pair, init)
    state = stage(state, buf0, 2 * qi, 0, buf1, 2 * qi + 1)
    state = stage(state, buf1, 2 * qi + 1, 1)
    ot = jnp.concatenate([acc[:HEAD_DIM] / acc[HEAD_DIM:HEAD_DIM + 1] for _, acc in state], axis=0)
    o_ref[...] = ot.T.astype(o_ref.dtype)


def _attention(qt, k, vt):
    _, s, _ = k.shape
    tq, tk = TQ_ATT, TILE_ATT
    assert tq == 2 * tk
    nk = s // tk
    kern = functools.partial(_attention_kernel, tq=tq, tk=tk)
    return pl.pallas_call(
        kern,
        grid=(N_HEADS // 2, s // tq),
        in_specs=[
            pl.BlockSpec((2, 1, LANES, tq), lambda hp, qi: (hp, qi, 0, 0)),
            pl.BlockSpec((2, s, LANES), lambda hp, qi: (hp, 0, 0)),
            pl.BlockSpec((2, nk, V_ROWS, tk), lambda hp, qi: (hp, 0, 0, 0)),
        ],
        out_specs=pl.BlockSpec((tq, 2 * HEAD_DIM), lambda hp, qi: (qi, hp)),
        out_shape=jax.ShapeDtypeStruct((s, N_HEADS * HEAD_DIM), BF16),
        scratch_shapes=[pltpu.VMEM((2, tk, tq), F32), pltpu.VMEM((2, tk, tq), F32),
                        pltpu.VMEM((2 * tk // SUB_K, 1, tq), F32),
                        pltpu.VMEM((2 * tk // SUB_K, 1, tq), F32)],
        compiler_params=pltpu.CompilerParams(
            dimension_semantics=("arbitrary", "arbitrary"), vmem_limit_bytes=VMEM_LIMIT),
        name="attention",
    )(qt, k, vt)


def _ffn_ple_kernel(*refs, tm, d_model, n_chunks, with_mix):
    if with_mix:
        (x_ref, ya_ref, yb_ref, woa_ref, wob_ref, g_ref, wup_ref, cv_ref, wdn_ref, p_ref,
         wproj_ref, wgate_ref, o_ref, halo_ref, acc_ref, xn_ref, h0_ref, h1_ref) = refs
    else:
        (x_ref, g_ref, wup_ref, cv_ref, wdn_ref, p_ref,
         wproj_ref, wgate_ref, o_ref, halo_ref, acc_ref, xn_ref, h0_ref, h1_ref) = refs
    i = pl.program_id(0)

    @pl.when(i == 0)
    def _():
        halo_ref[...] = jnp.zeros_like(halo_ref)

    x1 = x_ref[...]
    if with_mix:
        x1 = x1 + _dot(ya_ref[...], woa_ref[...]) + _dot(yb_ref[...], wob_ref[...])
    xn_ref[...] = (_rms(x1, d_model) * g_ref[...]).astype(BF16)
    acc_ref[...] = x1

    fc = FF_CHUNK

    def up_proj(h_ref, c):
        h_ref[SUBLANES:, :] = _dot(xn_ref[...], wup_ref[c])

    def process(h_ref, c):
        h_ref[:SUBLANES, :] = halo_ref[c]
        halo_ref[c] = h_ref[tm:, :]
        cv = cv_ref[c]
        hc = (cv[3:4] + cv[0:1] * h_ref[SUBLANES - 2:tm + SUBLANES - 2, :]
              + cv[1:2] * h_ref[SUBLANES - 1:tm + SUBLANES - 1, :] + cv[2:3] * h_ref[SUBLANES:, :])
        half_gate, up = 0.5 * hc[:, :fc], hc[:, fc:]
        act = ((half_gate + half_gate * jnp.tanh(half_gate)) * up).astype(BF16)
        acc_ref[...] += _dot(act, wdn_ref[c])

    def pair(jj, carry):
        c0 = 2 * jj
        up_proj(h1_ref, c0 + 1)
        process(h0_ref, c0)
        up_proj(h0_ref, c0 + 2)
        process(h1_ref, c0 + 1)
        return carry

    n_pairs = (n_chunks - 1) // 2
    up_proj(h0_ref, 0)
    lax.fori_loop(0, n_pairs, pair, 0)
    if n_chunks - 2 * n_pairs == 2:
        up_proj(h1_ref, n_chunks - 1)
        process(h0_ref, n_chunks - 2)
        process(h1_ref, n_chunks - 1)
    else:
        process(h0_ref, n_chunks - 1)

    x2 = acc_ref[...]
    gate = jax.nn.sigmoid(_dot(_rms(x2, d_model).astype(BF16), wgate_ref[...]))
    o_ref[...] = x2 + gate * _dot(p_ref[...].astype(BF16), wproj_ref[...])


def _ffn_ple(x, g_ffn, w_up, conv_w, conv_b, w_down, p, w_proj, w_gate, mix=None):
    s, d = x.shape
    d_ff = w_down.shape[0]
    fc = FF_CHUNK
    nc = d_ff // fc
    tm = TM_FFN
    wg = w_up[:, :d_ff].reshape(d, nc, fc)
    wu = w_up[:, d_ff:].reshape(d, nc, fc)
    wup = jnp.concatenate([wg, wu], axis=-1).transpose(1, 0, 2).astype(BF16)
    cv = jnp.concatenate([conv_w, conv_b[None]], axis=0)
    cv = jnp.concatenate([cv[:, :d_ff].reshape(4, nc, fc), cv[:, d_ff:].reshape(4, nc, fc)], axis=-1)
    cv = jnp.pad(cv.transpose(1, 0, 2), ((0, 0), (0, SUBLANES - 4), (0, 0)))
    wdn = w_down.reshape(nc, fc, d).astype(BF16)
    pdim = p.shape[-1]

    row = lambda i: (i, 0)
    args, specs = [x], [pl.BlockSpec((tm, d), row)]
    if mix is not None:
        ya, yb, w_out = mix
        wa = w_out[:ya.shape[1]].astype(BF16)
        wb = w_out[ya.shape[1]:].astype(BF16)
        args += [ya, yb, wa, wb]
        specs += [pl.BlockSpec((tm, ya.shape[1]), row), pl.BlockSpec((tm, yb.shape[1]), row),
                  _const_spec(wa.shape), _const_spec(wb.shape)]
    args += [g_ffn.reshape(1, d), wup, cv, wdn, p, w_proj.astype(BF16), w_gate.astype(BF16)]
    specs += [_const_spec((1, d)), _const_spec(wup.shape), _const_spec(cv.shape),
              _const_spec(wdn.shape), pl.BlockSpec((tm, pdim), row),
              _const_spec(w_proj.shape), _const_spec(w_gate.shape)]
    kern = functools.partial(_ffn_ple_kernel, tm=tm, d_model=d, n_chunks=nc,
                             with_mix=mix is not None)
    return pl.pallas_call(
        kern,
        grid=(s // tm,),
        in_specs=specs,
        out_specs=pl.BlockSpec((tm, d), row),
        out_shape=jax.ShapeDtypeStruct((s, d), F32),
        scratch_shapes=[pltpu.VMEM((nc, SUBLANES, 2 * fc), F32),
                        pltpu.VMEM((tm, d), F32),
                        pltpu.VMEM((tm, d), BF16),
                        pltpu.VMEM((tm + SUBLANES, 2 * fc), F32),
                        pltpu.VMEM((tm + SUBLANES, 2 * fc), F32)],
        compiler_params=pltpu.CompilerParams(
            dimension_semantics=("arbitrary",), vmem_limit_bytes=VMEM_LIMIT),
        name="ffn_ple_mix" if mix is not None else "ffn_ple",
    )(*args)


def _s5_prep_kernel(are_ref, aim_ref, ldt_ref, bre_ref, bim_ref,
                    abre_ref, abim_ref, bbre_ref, bbim_ref):
    lr, li = are_ref[...], aim_ref[...]
    dt = jnp.exp(ldt_ref[...])
    mag = jnp.exp(lr * dt)
    ab_re = mag * jnp.cos(li * dt)
    ab_im = mag * jnp.sin(li * dt)
    den = lr * lr + li * li
    nr, ni = ab_re - 1.0, ab_im
    cr = (nr * lr + ni * li) / den
    ci = (ni * lr - nr * li) / den
    br, bi = bre_ref[...], bim_ref[...]
    abre_ref[...] = ab_re
    abim_ref[...] = ab_im
    bbre_ref[...] = cr * br - ci * bi
    bbim_ref[...] = cr * bi + ci * br


def _s5_prep(a_re, a_im, log_dt, b_re, b_im):
    g, pst, c = b_re.shape
    shape2 = (g * pst * c // LANES, LANES)
    rep = lambda a: jnp.broadcast_to(a[:, :, None], (g, pst, c)).reshape(shape2)
    ldt = jnp.broadcast_to(log_dt[:, None, None], (g, pst, c)).reshape(shape2)
    out = jax.ShapeDtypeStruct(shape2, F32)
    ab_re, ab_im, bb_re, bb_im = pl.pallas_call(
        _s5_prep_kernel, out_shape=[out] * 4, name="s5_prep",
    )(rep(a_re), rep(a_im), ldt, b_re.reshape(shape2), b_im.reshape(shape2))
    ab_re = ab_re.reshape(g, pst, c)[:, :, 0]
    ab_im = ab_im.reshape(g, pst, c)[:, :, 0]
    return ab_re, ab_im, bb_re.reshape(g, pst, c), bb_im.reshape(g, pst, c)


def _s5_kernel(x_ref, g_ref, wbre_ref, wbim_ref, a_ref, cre_ref, cim_ref, d_ref, wglu_ref,
               o_ref, bre_scr, bim_scr, st_scr, *, t_tile, pitch, d_model, n_slab, n_lb):
    i = pl.program_id(0)

    @pl.when(i == 0)
    def _():
        st_scr[...] = jnp.zeros_like(st_scr)

    xt = x_ref[...]
    h = _rms(xt, d_model) * g_ref[...]
    rows8 = lax.broadcasted_iota(jnp.int32, (SUBLANES, LANES), 0)

    for j in range(n_slab):
        base = j * pitch
        sh = base % SUBLANES
        hj = h[:, j * LANES:(j + 1) * LANES]
        if sh:
            hj = pltpu.roll(hj, sh, 0)
        lhs = hj.astype(BF16)
        for w_ref, scr in ((wbre_ref, bre_scr), (wbim_ref, bim_scr)):
            bu = _dot(lhs, w_ref[j])
            for k in range(n_lb):
                blk = bu[:, k * LANES:(k + 1) * LANES]
                scr[k, pl.ds(base - sh, t_tile), :] = blk
                if sh:
                    scr[k, pl.ds(base - sh + t_tile, SUBLANES), :] = blk[:SUBLANES]

    ar = [a_ref[0, :, k * LANES:(k + 1) * LANES] for k in range(n_lb)]
    ai = [a_ref[1, :, k * LANES:(k + 1) * LANES] for k in range(n_lb)]

    def body(t, carry):
        xr, xi = carry
        nxr, nxi = [], []
        for k in range(n_lb):
            idx = (k, pl.ds(t, SUBLANES, stride=pitch), slice(None))
            nr = ar[k] * xr[k] - ai[k] * xi[k] + bre_scr[idx]
            ni = ar[k] * xi[k] + ai[k] * xr[k] + bim_scr[idx]
            bre_scr[idx] = nr
            bim_scr[idx] = ni
            nxr.append(nr)
            nxi.append(ni)
        return tuple(nxr), tuple(nxi)

    init = (tuple(st_scr[0, k] for k in range(n_lb)), tuple(st_scr[1, k] for k in range(n_lb)))
    xr, xi = lax.fori_loop(0, t_tile, body, init, unroll=SCAN_UNROLL)
    for k in range(n_lb):
        st_scr[0, k] = xr[k]
        st_scr[1, k] = xi[k]

    def slab_states(scr, base, sh):
        blocks = []
        for k in range(n_lb):
            main = scr[k, pl.ds(base - sh, t_tile), :]
            if sh:
                tail = scr[k, pl.ds(base - sh + t_tile, SUBLANES), :]
                first = jnp.where(rows8 < sh, tail, main[:SUBLANES])
                main = jnp.concatenate([first, main[SUBLANES:]], axis=0)
            blocks.append(main)
        return jnp.concatenate(blocks, axis=-1).astype(BF16)

    ys = []
    for j in range(n_slab):
        base = j * pitch
        sh = base % SUBLANES
        yj = (_dot(slab_states(bre_scr, base, sh), cre_ref[j])
              - _dot(slab_states(bim_scr, base, sh), cim_ref[j]))
        if sh:
            yj = pltpu.roll(yj, t_tile - sh, 0)
        ys.append(yj)
    y = jnp.concatenate(ys, axis=-1) + d_ref[...] * h
    gl = _dot(jax.nn.gelu(y).astype(BF16), wglu_ref[...])
    o_ref[...] = xt + gl[:, :d_model] * jax.nn.sigmoid(gl[:, d_model:])


def _s5_layer(x, g_mix, a_re, a_im, log_dt, b_re, b_im, c_re, c_im, d_skip, w_glu):
    s, d = x.shape
    n_groups = d // S5_GROUP_CH
    gps = LANES // S5_GROUP_CH
    n_slab = d // LANES
    assert n_slab == SUBLANES
    slab_states = gps * S5_STATE
    n_lb = slab_states // LANES
    t_tile, pitch = T_S5, S5_PITCH

    ab_re, ab_im, bb_re, bb_im = _s5_prep(a_re, a_im, log_dt, b_re, b_im)
    eye = jnp.eye(gps, dtype=F32)

    def bmat(bb):
        bb = bb.reshape(n_slab, gps, S5_STATE, S5_GROUP_CH)
        w = jnp.einsum('jgpc,gh->jgchp', bb, eye)
        return w.reshape(n_slab, LANES, slab_states).astype(BF16)

    def cmat(cc):
        cc = cc.reshape(n_slab, gps, S5_GROUP_CH, S5_STATE)
        w = jnp.einsum('jgcp,gh->jgphc', cc, eye)
        return w.reshape(n_slab, slab_states, LANES).astype(BF16)

    a = jnp.stack([ab_re.reshape(n_slab, slab_states), ab_im.reshape(n_slab, slab_states)])
    wbre, wbim, cre, cim = bmat(bb_re), bmat(bb_im), cmat(c_re), cmat(c_im)
    wglu = w_glu.astype(BF16)

    row = lambda i: (i, 0)
    kern = functools.partial(_s5_kernel, t_tile=t_tile, pitch=pitch, d_model=d,
                             n_slab=n_slab, n_lb=n_lb)
    return pl.pallas_call(
        kern,
        grid=(s // t_tile,),
        in_specs=[
            pl.BlockSpec((t_tile, d), row),
            _const_spec((1, d)),
            _const_spec(wbre.shape), _const_spec(wbim.shape), _const_spec(a.shape),
            _const_spec(cre.shape), _const_spec(cim.shape), _const_spec((1, d)),
            _const_spec(wglu.shape),
        ],
        out_specs=pl.BlockSpec((t_tile, d), row),
        out_shape=jax.ShapeDtypeStruct((s, d), F32),
        scratch_shapes=[pltpu.VMEM((n_lb, n_slab * pitch, LANES), F32),
                        pltpu.VMEM((n_lb, n_slab * pitch, LANES), F32),
                        pltpu.VMEM((2, n_lb, SUBLANES, LANES), F32)],
        compiler_params=pltpu.CompilerParams(
            dimension_semantics=("arbitrary",), vmem_limit_bytes=VMEM_LIMIT),
        name="s5_layer",
    )(x, g_mix.reshape(1, d), wbre, wbim, a, cre, cim, d_skip.reshape(1, d), wglu)


def kernel(x, p, norm_mix, norm_ffn, ev_w_in, ev_b_fgate, ev_q_norm, ev_k_norm, ev_v_norm,
           ev_w_spatial, ev_b_spatial, ev_w_out, od_a_re, od_a_im, od_log_dt, od_b_re, od_b_im,
           od_c_re, od_c_im, od_d, od_w_glu, ffn_w_up, ffn_conv_w, ffn_conv_b, ffn_w_down,
           ple_w_proj, ple_w_gate):
    bsz, seq, d = x.shape
    depth = p.shape[0]
    outs = []
    for b in range(bsz):
        xb = x[b]
        for i in range(depth):
            mix = None
            if i % 2 == 0:
                e = i // 2
                ya, qt, k, vt = _even_pre(xb, norm_mix[i], ev_w_in[e], ev_b_fgate[e], ev_q_norm[e],
                                          ev_k_norm[e], ev_v_norm[e], ev_w_spatial[e], ev_b_spatial[e])
                yb = _attention(qt, k, vt)
                mix = (ya, yb, ev_w_out[e])
            else:
                o = i // 2
                xb = _s5_layer(xb, norm_mix[i], od_a_re[o], od_a_im[o], od_log_dt[o], od_b_re[o],
                               od_b_im[o], od_c_re[o], od_c_im[o], od_d[o], od_w_glu[o])
            xb = _ffn_ple(xb, norm_ffn[i], ffn_w_up[i], ffn_conv_w[i], ffn_conv_b[i],
                          ffn_w_down[i], p[i, b], ple_w_proj[i], ple_w_gate[i], mix=mix)
        outs.append(xb)
    return jnp.stack(outs)
```

```python
import functools

import jax
import jax.numpy as jnp
import numpy as np
from jax import lax
from jax.experimental import pallas as pl
from jax.experimental.pallas import tpu as pltpu

F32 = jnp.float32
BF16 = jnp.bfloat16
EPS = 1e-6
NEG_INF = -1e30
LANES = 128
SUBLANES = 8
VMEM_LIMIT = 56 * 1024 * 1024

A_GROUPS = 4
A_CHUNK = 128
N_HEADS = 8
HEAD_DIM = 64
S5_GROUP_CH = 16
S5_STATE = 64
CONV_W = 3

LOG2E = 1.4426950408889634
V_ROWS = 80

TILE_ATT = 512
TQ_ATT = 1024
SUB_K = 256
TM_FFN = 512
FF_CHUNK = 256
T_S5 = 256
S5_PITCH = T_S5 + 4
SCAN_UNROLL = 4


def _const_spec(shape):
    n = len(shape)
    return pl.BlockSpec(shape, lambda *_: (0,) * n, pipeline_mode=pl.Buffered(1))


def _layer_spec(shape, layer):
    n = len(shape) - 1
    return pl.BlockSpec((None,) + tuple(shape[1:]), lambda *_: (layer,) + (0,) * n,
                        pipeline_mode=pl.Buffered(1))


def _rms(x, n):
    return x * lax.rsqrt(jnp.sum(x * x, axis=-1, keepdims=True) * (1.0 / n) + EPS)


def _dot(a, b):
    return jnp.dot(a, b, preferred_element_type=F32)


def _split3(x):
    hi = x.astype(BF16)
    r = x - hi.astype(F32)
    mid = r.astype(BF16)
    lo = (r - mid.astype(F32)).astype(BF16)
    return hi, mid, lo


def _even_pre_kernel(x_ref, g_ref, wuv_ref, wq_ref, wk_ref, wv_ref, wf_ref, bf_ref,
                     qg_ref, kg_ref, qc_ref, vc_ref, vgain_ref, wsp_ref, bsp_ref, e_ref,
                     ya_ref, qt_ref, k_ref, vt_ref, carry_ref, *, tm, d_model, a_width):
    i = pl.program_id(0)

    @pl.when(i == 0)
    def _():
        carry_ref[...] = jnp.zeros_like(carry_ref)

    h = (_rms(x_ref[...], d_model) * g_ref[...]).astype(BF16)

    zuv = _dot(h, wuv_ref[...])
    u = jax.nn.gelu(zuv[:, :a_width])
    v = jax.nn.gelu(zuv[:, a_width:])
    tri = (lax.broadcasted_iota(jnp.int32, (A_CHUNK, A_CHUNK), 1)
           <= lax.broadcasted_iota(jnp.int32, (A_CHUNK, A_CHUNK), 0))
    for g in range(A_GROUPS):
        cols = slice(g * LANES, (g + 1) * LANES)
        vg = (_rms(v[:, cols], LANES) * vgain_ref[:, cols]).astype(BF16)
        w = jnp.where(tri, wsp_ref[g], 0.0).astype(BF16)
        for c in range(tm // A_CHUNK):
            rows = slice(c * A_CHUNK, (c + 1) * A_CHUNK)
            sv = _dot(w, vg[rows]) + bsp_ref[g]
            ya_ref[rows, cols] = (u[rows, cols] * sv).astype(BF16)

    f = _dot(h, wf_ref[...]) + bf_ref[...]
    ls = jnp.minimum(f, 0.0) - jnp.log1p(jnp.exp(-jnp.abs(f)))
    tri_t = (lax.broadcasted_iota(jnp.int32, (tm, tm), 1)
             <= lax.broadcasted_iota(jnp.int32, (tm, tm), 0)).astype(BF16)
    hi, mid, lo = _split3(ls)
    cum = _dot(tri_t, hi) + _dot(tri_t, mid) + _dot(tri_t, lo) + carry_ref[0:1, :]
    carry_ref[...] = jnp.broadcast_to(cum[tm - 1:tm, :], carry_ref.shape)
    chi, cmid, clo = _split3(cum * LOG2E)
    kadd = _dot(chi, e_ref[0]) + _dot(cmid, e_ref[1]) + _dot(clo, e_ref[2])

    zq = _dot(h, wq_ref[...])
    zk = _dot(h, wk_ref[...])
    zv = _dot(h, wv_ref[...])
    for hd in range(N_HEADS):
        cols = slice(hd * LANES, (hd + 1) * LANES)
        qn = _rms(zq[:, cols], HEAD_DIM) * qg_ref[...] + qc_ref[...]
        qt_ref[hd, 0] = qn.T.astype(BF16)
        k_ref[hd] = (_rms(zk[:, cols], HEAD_DIM) * kg_ref[...] + kadd[:, cols]).astype(BF16)
        vt_ref[hd, 0] = (zv[:, cols] + vc_ref[...]).T[:V_ROWS].astype(BF16)


def _prep_even(ev_w_in, ev_b_fgate, ev_q_norm, ev_k_norm, ev_v_norm, ev_b_spatial, ev_w_out):
    n, d, _ = ev_w_in.shape
    a_width = A_GROUPS * LANES
    b_width = N_HEADS * HEAD_DIM
    o = 2 * a_width
    wuv = ev_w_in[:, :, :o].astype(BF16)
    qkv = ev_w_in[:, :, o:o + 3 * b_width].reshape(n, d, 3, N_HEADS, HEAD_DIM).transpose(0, 2, 1, 3, 4)
    qkv = jnp.pad(qkv, ((0, 0),) * 4 + ((0, LANES - HEAD_DIM),))
    wqkv = qkv.reshape(n, 3, d, N_HEADS * LANES).astype(BF16)
    wf = jnp.pad(ev_w_in[:, :, o + 3 * b_width:], ((0, 0), (0, 0), (0, LANES - N_HEADS))).astype(BF16)
    bf = jnp.pad(ev_b_fgate, ((0, 0), (0, LANES - N_HEADS))).reshape(n, 1, LANES)
    pad = ((0, 0), (0, LANES - HEAD_DIM))
    qg = jnp.pad(ev_q_norm * (HEAD_DIM ** -0.5 * LOG2E), pad).reshape(n, 1, LANES)
    kg = jnp.pad(ev_k_norm, pad).reshape(n, 1, LANES)
    bsp = jnp.broadcast_to(ev_b_spatial[..., None], ev_b_spatial.shape + (LANES,))
    return dict(wuv=wuv, wqkv=wqkv, wf=wf, bf=bf, qg=qg, kg=kg,
                vgain=ev_v_norm.reshape(n, 1, a_width), bsp=bsp, wo=ev_w_out.astype(BF16))


def _even_consts():
    lane = np.arange(LANES)
    qc = np.where((lane >= HEAD_DIM) & (lane < HEAD_DIM + 3), -1.0, 0.0).astype(np.float32).reshape(1, LANES)
    vc = (lane == HEAD_DIM).astype(np.float32).reshape(1, LANES)
    e = np.zeros((3, LANES, N_HEADS * LANES), np.float32)
    hh = np.arange(N_HEADS)
    for part in range(3):
        e[part, hh, hh * LANES + HEAD_DIM + part] = 1.0
    return jnp.asarray(qc), jnp.asarray(vc), jnp.asarray(e, dtype=BF16)


def _even_pre(x, g_all, layer, ev, e_idx, w_sp_all):
    s, d = x.shape
    a_width = A_GROUPS * LANES
    tm = TILE_ATT
    qc, vc, e = _even_consts()
    n3 = lambda j: pl.BlockSpec((None, None, d, N_HEADS * LANES), lambda i: (e_idx, j, 0, 0),
                                pipeline_mode=pl.Buffered(1))
    row = lambda i: (i, 0)
    hrow = lambda i: (0, i, 0)
    trow = lambda i: (0, i, 0, 0)
    qsplit = TQ_ATT // tm
    kern = functools.partial(_even_pre_kernel, tm=tm, d_model=d, a_width=a_width)
    return pl.pallas_call(
        kern,
        grid=(s // tm,),
        in_specs=[
            pl.BlockSpec((tm, d), row),
            _layer_spec(g_all.shape, layer),
            _layer_spec(ev["wuv"].shape, e_idx), n3(0), n3(1), n3(2),
            _layer_spec(ev["wf"].shape, e_idx), _layer_spec(ev["bf"].shape, e_idx),
            _layer_spec(ev["qg"].shape, e_idx), _layer_spec(ev["kg"].shape, e_idx),
            _const_spec(qc.shape), _const_spec(vc.shape),
            _layer_spec(ev["vgain"].shape, e_idx), _layer_spec(w_sp_all.shape, e_idx),
            _layer_spec(ev["bsp"].shape, e_idx), _const_spec(e.shape),
        ],
        out_specs=[
            pl.BlockSpec((tm, a_width), row),
            pl.BlockSpec((N_HEADS, 1, LANES, tm), lambda i: (0, i // qsplit, 0, i % qsplit)),
            pl.BlockSpec((N_HEADS, tm, LANES), hrow),
            pl.BlockSpec((N_HEADS, 1, V_ROWS, tm), trow),
        ],
        out_shape=[jax.ShapeDtypeStruct((s, a_width), BF16),
                   jax.ShapeDtypeStruct((N_HEADS, s // TQ_ATT, LANES, TQ_ATT), BF16),
                   jax.ShapeDtypeStruct((N_HEADS, s, LANES), BF16),
                   jax.ShapeDtypeStruct((N_HEADS, s // tm, V_ROWS, tm), BF16)],
        scratch_shapes=[pltpu.VMEM((SUBLANES, LANES), F32)],
        compiler_params=pltpu.CompilerParams(
            dimension_semantics=("arbitrary",), vmem_limit_bytes=VMEM_LIMIT),
        name="even_pre",
    )(x, g_all, ev["wuv"], ev["wqkv"], ev["wqkv"], ev["wqkv"], ev["wf"], ev["bf"], ev["qg"], ev["kg"],
      qc, vc, ev["vgain"], w_sp_all, ev["bsp"], e)


def _attention_kernel(qt_ref, k_ref, vt_ref, o_ref, s0_ref, s1_ref, mx0_ref, mx1_ref, *, tq, tk):
    qi = pl.program_id(1)
    krow = lax.broadcasted_iota(jnp.int32, (SUB_K, tq), 0)
    qcol = lax.broadcasted_iota(jnp.int32, (SUB_K, tq), 1)
    n_sub = tk // SUB_K

    pieces = [(sub, hh) for sub in range(n_sub) for hh in range(2)]

    def scores(buf, kv, sub, hh):
        s_ref, mx_ref = buf
        start = pl.multiple_of(kv * tk, tk)
        kc = k_ref[hh, pl.ds(start + sub * SUB_K, SUB_K), :]
        st = _dot(kc, qt_ref[hh, 0])
        s_ref[hh, sub * SUB_K:(sub + 1) * SUB_K, :] = st
        mx_ref[sub * 2 + hh] = jnp.max(st, axis=0, keepdims=True)

    def softmax_pv(buf, kv, state, diag, sub, hh):
        s_ref, mx_ref = buf
        m, acc = state[hh]
        st = s_ref[hh, sub * SUB_K:(sub + 1) * SUB_K, :]
        if diag is None:
            mx = mx_ref[sub * 2 + hh]
        else:
            st = jnp.where(krow + (diag * tk + sub * SUB_K) <= qcol, st, NEG_INF)
            mx = jnp.max(st, axis=0, keepdims=True)
        m_new = jnp.maximum(m, mx)
        alpha = jnp.exp2(m - m_new)
        pt = jnp.exp2(st - m_new).astype(BF16)
        vt = vt_ref[hh, kv][:, sub * SUB_K:(sub + 1) * SUB_K]
        state[hh] = (m_new, acc * alpha + _dot(vt, pt))

    def stage(state, cur, cur_kv, diag, nxt=None, nxt_kv=None):
        state = list(state)
        for sub, hh in pieces:
            if nxt is not None:
                scores(nxt, nxt_kv, sub, hh)
            softmax_pv(cur, cur_kv, state, diag, sub, hh)
        return tuple(state)

    buf0, buf1 = (s0_ref, mx0_ref), (s1_ref, mx1_ref)

    def pair(jj, state):
        c0 = 2 * jj
        state = stage(state, buf0, c0, None, buf1, c0 + 1)
        return stage(state, buf1, c0 + 1, None, buf0, c0 + 2)

    init = tuple((jnp.full((1, tq), NEG_INF, F32), jnp.zeros((V_ROWS, tq), F32)) for _ in range(2))
    for sub, hh in pieces:
        scores(buf0, 0, sub, hh)
    state = lax.fori_loop(0, qi, pair, init)
    state = stage(state, buf0, 2 * qi, 0, buf1, 2 * qi + 1)
    state = stage(state, buf1, 2 * qi + 1, 1)
    ot = jnp.concatenate([acc[:HEAD_DIM] / acc[HEAD_DIM:HEAD_DIM + 1] for _, acc in state], axis=0)
    o_ref[...] = ot.T.astype(o_ref.dtype)


def _attention(qt, k, vt):
    _, s, _ = k.shape
    tq, tk = TQ_ATT, TILE_ATT
    assert tq == 2 * tk
    nk = s // tk
    kern = functools.partial(_attention_kernel, tq=tq, tk=tk)
    return pl.pallas_call(
        kern,
        grid=(N_HEADS // 2, s // tq),
        in_specs=[
            pl.BlockSpec((2, 1, LANES, tq), lambda hp, qi: (hp, qi, 0, 0)),
            pl.BlockSpec((2, s, LANES), lambda hp, qi: (hp, 0, 0)),
            pl.BlockSpec((2, nk, V_ROWS, tk), lambda hp, qi: (hp, 0, 0, 0)),
        ],
        out_specs=pl.BlockSpec((tq, 2 * HEAD_DIM), lambda hp, qi: (qi, hp)),
        out_shape=jax.ShapeDtypeStruct((s, N_HEADS * HEAD_DIM), BF16),
        scratch_shapes=[pltpu.VMEM((2, tk, tq), F32), pltpu.VMEM((2, tk, tq), F32),
                        pltpu.VMEM((2 * tk // SUB_K, 1, tq), F32),
                        pltpu.VMEM((2 * tk // SUB_K, 1, tq), F32)],
        compiler_params=pltpu.CompilerParams(
            dimension_semantics=("arbitrary", "arbitrary"), vmem_limit_bytes=VMEM_LIMIT),
        name="attention",
    )(qt, k, vt)


def _ffn_ple_kernel(*refs, tm, d_model, n_chunks, with_mix):
    if with_mix:
        (x_ref, ya_ref, yb_ref, woa_ref, wob_ref, g_ref, wup_ref, cv_ref, wdn_ref, p_ref,
         wproj_ref, wgate_ref, o_ref, halo_ref, acc_ref, xn_ref, h0_ref, h1_ref) = refs
    else:
        (x_ref, g_ref, wup_ref, cv_ref, wdn_ref, p_ref,
         wproj_ref, wgate_ref, o_ref, halo_ref, acc_ref, xn_ref, h0_ref, h1_ref) = refs
    i = pl.program_id(0)

    @pl.when(i == 0)
    def _():
        halo_ref[...] = jnp.zeros_like(halo_ref)

    x1 = x_ref[...]
    if with_mix:
        x1 = x1 + _dot(ya_ref[...], woa_ref[...]) + _dot(yb_ref[...], wob_ref[...])
    xn_ref[...] = (_rms(x1, d_model) * g_ref[...]).astype(BF16)
    acc_ref[...] = x1

    fc = FF_CHUNK

    def up_proj(h_ref, c):
        h_ref[SUBLANES:, :] = _dot(xn_ref[...], wup_ref[c])

    def process(h_ref, c):
        h_ref[:SUBLANES, :] = halo_ref[c]
        halo_ref[c] = h_ref[tm:, :]
        cv = cv_ref[c]
        hc = (cv[3:4] + cv[0:1] * h_ref[SUBLANES - 2:tm + SUBLANES - 2, :]
              + cv[1:2] * h_ref[SUBLANES - 1:tm + SUBLANES - 1, :] + cv[2:3] * h_ref[SUBLANES:, :])
        half_gate, up = 0.5 * hc[:, :fc], hc[:, fc:]
        act = ((half_gate + half_gate * jnp.tanh(half_gate)) * up).astype(BF16)
        acc_ref[...] += _dot(act, wdn_ref[c])

    def pair(jj, carry):
        c0 = 2 * jj
        up_proj(h1_ref, c0 + 1)
        process(h0_ref, c0)
        up_proj(h0_ref, c0 + 2)
        process(h1_ref, c0 + 1)
        return carry

    n_pairs = (n_chunks - 1) // 2
    up_proj(h0_ref, 0)
    lax.fori_loop(0, n_pairs, pair, 0)
    if n_chunks - 2 * n_pairs == 2:
        up_proj(h1_ref, n_chunks - 1)
        process(h0_ref, n_chunks - 2)
        process(h1_ref, n_chunks - 1)
    else:
        process(h0_ref, n_chunks - 1)

    x2 = acc_ref[...]
    gate = jax.nn.sigmoid(_dot(_rms(x2, d_model).astype(BF16), wgate_ref[...]))
    o_ref[...] = x2 + gate * _dot(p_ref[...].astype(BF16), wproj_ref[...])


def _prep_ffn(ffn_w_up, ffn_conv_w, ffn_conv_b, ffn_w_down, ple_w_proj, ple_w_gate):
    n, d, two_ff = ffn_w_up.shape
    d_ff = two_ff // 2
    fc = FF_CHUNK
    nc = d_ff // fc
    wup = ffn_w_up.reshape(n, d, 2, nc, fc).transpose(0, 3, 1, 2, 4).reshape(n, nc, d, 2 * fc).astype(BF16)
    cv = jnp.concatenate([ffn_conv_w, ffn_conv_b[:, None]], axis=1)
    cv = cv.reshape(n, CONV_W + 1, 2, nc, fc).transpose(0, 3, 1, 2, 4).reshape(n, nc, CONV_W + 1, 2 * fc)
    cv = jnp.pad(cv, ((0, 0), (0, 0), (0, SUBLANES - CONV_W - 1), (0, 0)))
    return dict(wup=wup, cv=cv, wdn=ffn_w_down.reshape(n, nc, fc, d).astype(BF16),
                wproj=ple_w_proj.astype(BF16), wgate=ple_w_gate.astype(BF16))


def _ffn_ple(x, g_all, layer, ff, p_all, batch, mix=None):
    s, d = x.shape
    _, nc, _, two_fc = ff["wup"].shape
    fc = two_fc // 2
    tm = TM_FFN
    pdim = p_all.shape[-1]

    row = lambda i: (i, 0)
    args, specs = [x], [pl.BlockSpec((tm, d), row)]
    if mix is not None:
        ya, yb, wo_all, e_idx = mix
        aw = ya.shape[1]
        half = lambda j: pl.BlockSpec((None, aw, d), lambda i: (e_idx, j, 0), pipeline_mode=pl.Buffered(1))
        args += [ya, yb, wo_all, wo_all]
        specs += [pl.BlockSpec((tm, aw), row), pl.BlockSpec((tm, yb.shape[1]), row), half(0), half(1)]
    args += [g_all, ff["wup"], ff["cv"], ff["wdn"], p_all, ff["wproj"], ff["wgate"]]
    specs += [_layer_spec(g_all.shape, layer), _layer_spec(ff["wup"].shape, layer),
              _layer_spec(ff["cv"].shape, layer), _layer_spec(ff["wdn"].shape, layer),
              pl.BlockSpec((None, None, tm, pdim), lambda i: (layer, batch, i, 0)),
              _layer_spec(ff["wproj"].shape, layer), _layer_spec(ff["wgate"].shape, layer)]
    kern = functools.partial(_ffn_ple_kernel, tm=tm, d_model=d, n_chunks=nc,
                             with_mix=mix is not None)
    return pl.pallas_call(
        kern,
        grid=(s // tm,),
        in_specs=specs,
        out_specs=pl.BlockSpec((tm, d), row),
        out_shape=jax.ShapeDtypeStruct((s, d), F32),
        scratch_shapes=[pltpu.VMEM((nc, SUBLANES, 2 * fc), F32),
                        pltpu.VMEM((tm, d), F32),
                        pltpu.VMEM((tm, d), BF16),
                        pltpu.VMEM((tm + SUBLANES, 2 * fc), F32),
                        pltpu.VMEM((tm + SUBLANES, 2 * fc), F32)],
        compiler_params=pltpu.CompilerParams(
            dimension_semantics=("arbitrary",), vmem_limit_bytes=VMEM_LIMIT),
        name="ffn_ple_mix" if mix is not None else "ffn_ple",
    )(*args)


def _s5_prep_kernel(are_ref, aim_ref, ldt_ref, bre_ref, bim_ref,
                    abre_ref, abim_ref, bbre_ref, bbim_ref):
    lr, li = are_ref[...], aim_ref[...]
    dt = jnp.exp(ldt_ref[...])
    mag = jnp.exp(lr * dt)
    ab_re = mag * jnp.cos(li * dt)
    ab_im = mag * jnp.sin(li * dt)
    den = lr * lr + li * li
    nr, ni = ab_re - 1.0, ab_im
    cr = (nr * lr + ni * li) / den
    ci = (ni * lr - nr * li) / den
    br, bi = bre_ref[...], bim_ref[...]
    abre_ref[...] = ab_re
    abim_ref[...] = ab_im
    bbre_ref[...] = cr * br - ci * bi
    bbim_ref[...] = cr * bi + ci * br


def _prep_odd(od_a_re, od_a_im, od_log_dt, od_b_re, od_b_im, od_c_re, od_c_im, od_w_glu):
    n, g, pst, c = od_b_re.shape
    shape2 = (n * g * pst * c // LANES, LANES)
    full = (n, g, pst, c)
    rep = lambda a: jnp.broadcast_to(a[..., None], full).reshape(shape2)
    ldt = jnp.broadcast_to(od_log_dt[:, :, None, None], full).reshape(shape2)
    out = jax.ShapeDtypeStruct(shape2, F32)
    ab_re, ab_im, bb_re, bb_im = pl.pallas_call(
        _s5_prep_kernel, out_shape=[out] * 4, name="s5_prep",
    )(rep(od_a_re), rep(od_a_im), ldt, od_b_re.reshape(shape2), od_b_im.reshape(shape2))
    ab_re = ab_re.reshape(full)[..., 0]
    ab_im = ab_im.reshape(full)[..., 0]

    gps = LANES // S5_GROUP_CH
    n_slab = g // gps
    slab_states = gps * pst
    eye = jnp.eye(gps, dtype=F32)

    def bmat(bb):
        bb = bb.reshape(n, n_slab, gps, pst, c)
        w = jnp.einsum('njgpc,gh->njgchp', bb, eye)
        return w.reshape(n, n_slab, LANES, slab_states).astype(BF16)

    def cmat(cc):
        cc = cc.reshape(n, n_slab, gps, c, pst)
        w = jnp.einsum('njgcp,gh->njgphc', cc, eye)
        return w.reshape(n, n_slab, slab_states, LANES).astype(BF16)

    a = jnp.stack([ab_re.reshape(n, n_slab, slab_states), ab_im.reshape(n, n_slab, slab_states)], axis=1)
    return dict(a=a, wbre=bmat(bb_re.reshape(full)), wbim=bmat(bb_im.reshape(full)),
                cre=cmat(od_c_re), cim=cmat(od_c_im), wglu=od_w_glu.astype(BF16))


def _s5_kernel(x_ref, g_ref, wbre_ref, wbim_ref, a_ref, cre_ref, cim_ref, d_ref, wglu_ref,
               o_ref, bre_scr, bim_scr, st_scr, *, t_tile, pitch, d_model, n_slab, n_lb):
    i = pl.program_id(0)

    @pl.when(i == 0)
    def _():
        st_scr[...] = jnp.zeros_like(st_scr)

    xt = x_ref[...]
    h = _rms(xt, d_model) * g_ref[...]
    rows8 = lax.broadcasted_iota(jnp.int32, (SUBLANES, LANES), 0)

    for j in range(n_slab):
        base = j * pitch
        sh = base % SUBLANES
        hj = h[:, j * LANES:(j + 1) * LANES]
        if sh:
            hj = pltpu.roll(hj, sh, 0)
        lhs = hj.astype(BF16)
        for w_ref, scr in ((wbre_ref, bre_scr), (wbim_ref, bim_scr)):
            bu = _dot(lhs, w_ref[j])
            for k in range(n_lb):
                blk = bu[:, k * LANES:(k + 1) * LANES]
                scr[k, pl.ds(base - sh, t_tile), :] = blk
                if sh:
                    scr[k, pl.ds(base - sh + t_tile, SUBLANES), :] = blk[:SUBLANES]

    ar = [a_ref[0, :, k * LANES:(k + 1) * LANES] for k in range(n_lb)]
    ai = [a_ref[1, :, k * LANES:(k + 1) * LANES] for k in range(n_lb)]

    def body(t, carry):
        xr, xi = carry
        nxr, nxi = [], []
        for k in range(n_lb):
            idx = (k, pl.ds(t, SUBLANES, stride=pitch), slice(None))
            nr = ar[k] * xr[k] - ai[k] * xi[k] + bre_scr[idx]
            ni = ar[k] * xi[k] + ai[k] * xr[k] + bim_scr[idx]
            bre_scr[idx] = nr
            bim_scr[idx] = ni
            nxr.append(nr)
            nxi.append(ni)
        return tuple(nxr), tuple(nxi)

    init = (tuple(st_scr[0, k] for k in range(n_lb)), tuple(st_scr[1, k] for k in range(n_lb)))
    xr, xi = lax.fori_loop(0, t_tile, body, init, unroll=SCAN_UNROLL)
    for k in range(n_lb):
        st_scr[0, k] = xr[k]
        st_scr[1, k] = xi[k]

    def slab_states(scr, base, sh):
        blocks = []
        for k in range(n_lb):
            main = scr[k, pl.ds(base - sh, t_tile), :]
            if sh:
                tail = scr[k, pl.ds(base - sh + t_tile, SUBLANES), :]
                first = jnp.where(rows8 < sh, tail, main[:SUBLANES])
                main = jnp.concatenate([first, main[SUBLANES:]], axis=0)
            blocks.append(main)
        return jnp.concatenate(blocks, axis=-1).astype(BF16)

    ys = []
    for j in range(n_slab):
        base = j * pitch
        sh = base % SUBLANES
        yj = (_dot(slab_states(bre_scr, base, sh), cre_ref[j])
              - _dot(slab_states(bim_scr, base, sh), cim_ref[j]))
        if sh:
            yj = pltpu.roll(yj, t_tile - sh, 0)
        ys.append(yj)
    y = jnp.concatenate(ys, axis=-1) + d_ref[...] * h
    gl = _dot(jax.nn.gelu(y).astype(BF16), wglu_ref[...])
    o_ref[...] = xt + gl[:, :d_model] * jax.nn.sigmoid(gl[:, d_model:])


def _s5_layer(x, g_all, layer, od, o_idx, d_all):
    s, d = x.shape
    _, n_slab, _, slab_states = od["wbre"].shape
    assert n_slab == SUBLANES
    n_lb = slab_states // LANES
    t_tile, pitch = T_S5, S5_PITCH

    row = lambda i: (i, 0)
    kern = functools.partial(_s5_kernel, t_tile=t_tile, pitch=pitch, d_model=d,
                             n_slab=n_slab, n_lb=n_lb)
    return pl.pallas_call(
        kern,
        grid=(s // t_tile,),
        in_specs=[
            pl.BlockSpec((t_tile, d), row),
            _layer_spec(g_all.shape, layer),
            _layer_spec(od["wbre"].shape, o_idx), _layer_spec(od["wbim"].shape, o_idx),
            _layer_spec(od["a"].shape, o_idx),
            _layer_spec(od["cre"].shape, o_idx), _layer_spec(od["cim"].shape, o_idx),
            _layer_spec(d_all.shape, o_idx), _layer_spec(od["wglu"].shape, o_idx),
        ],
        out_specs=pl.BlockSpec((t_tile, d), row),
        out_shape=jax.ShapeDtypeStruct((s, d), F32),
        scratch_shapes=[pltpu.VMEM((n_lb, n_slab * pitch, LANES), F32),
                        pltpu.VMEM((n_lb, n_slab * pitch, LANES), F32),
                        pltpu.VMEM((2, n_lb, SUBLANES, LANES), F32)],
        compiler_params=pltpu.CompilerParams(
            dimension_semantics=("arbitrary",), vmem_limit_bytes=VMEM_LIMIT),
        name="s5_layer",
    )(x, g_all, od["wbre"], od["wbim"], od["a"], od["cre"], od["cim"], d_all, od["wglu"])


def kernel(x, p, norm_mix, norm_ffn, ev_w_in, ev_b_fgate, ev_q_norm, ev_k_norm, ev_v_norm,
           ev_w_spatial, ev_b_spatial, ev_w_out, od_a_re, od_a_im, od_log_dt, od_b_re, od_b_im,
           od_c_re, od_c_im, od_d, od_w_glu, ffn_w_up, ffn_conv_w, ffn_conv_b, ffn_w_down,
           ple_w_proj, ple_w_gate):
    bsz, seq, d = x.shape
    depth = p.shape[0]
    ev = _prep_even(ev_w_in, ev_b_fgate, ev_q_norm, ev_k_norm, ev_v_norm, ev_b_spatial, ev_w_out)
    od = _prep_odd(od_a_re, od_a_im, od_log_dt, od_b_re, od_b_im, od_c_re, od_c_im, od_w_glu)
    ff = _prep_ffn(ffn_w_up, ffn_conv_w, ffn_conv_b, ffn_w_down, ple_w_proj, ple_w_gate)
    g_mix = norm_mix.reshape(depth, 1, d)
    g_ffn = norm_ffn.reshape(depth, 1, d)
    d_skip = od_d.reshape(-1, 1, d)
    outs = []
    for b in range(bsz):
        xb = x[b]
        for i in range(depth):
            mix = None
            if i % 2 == 0:
                e = i // 2
                ya, qt, k, vt = _even_pre(xb, g_mix, i, ev, e, ev_w_spatial)
                mix = (ya, _attention(qt, k, vt), ev["wo"], e)
            else:
                xb = _s5_layer(xb, g_mix, i, od, i // 2, d_skip)
            xb = _ffn_ple(xb, g_ffn, i, ff, p, b, mix=mix)
        outs.append(xb)
    return outs[0][None] if bsz == 1 else jnp.stack(outs)
```

```python
import functools

import jax
import jax.numpy as jnp
import numpy as np
from jax import lax
from jax.experimental import pallas as pl
from jax.experimental.pallas import tpu as pltpu

F32 = jnp.float32
BF16 = jnp.bfloat16
EPS = 1e-6
NEG_INF = -1e30
LANES = 128
SUBLANES = 8
VMEM_LIMIT = 56 * 1024 * 1024

A_GROUPS = 4
A_CHUNK = 128
N_HEADS = 8
HEAD_DIM = 64
S5_GROUP_CH = 16
S5_STATE = 64
CONV_W = 3

LOG2E = 1.4426950408889634
V_ROWS = 80

TILE_ATT = 512
TQ_ATT = 1024
SUB_K = 256
TM_FFN = 512
FF_CHUNK = 256
T_S5 = 512
S5_PITCH = T_S5 + 4
SCAN_UNROLL = 4


def _const_spec(shape):
    n = len(shape)
    return pl.BlockSpec(shape, lambda *_: (0,) * n, pipeline_mode=pl.Buffered(1))


def _layer_spec(shape, layer):
    n = len(shape) - 1
    return pl.BlockSpec((None,) + tuple(shape[1:]), lambda *_: (layer,) + (0,) * n,
                        pipeline_mode=pl.Buffered(1))


def _rms(x, n):
    return x * lax.rsqrt(jnp.sum(x * x, axis=-1, keepdims=True) * (1.0 / n) + EPS)


def _dot(a, b):
    return jnp.dot(a, b, preferred_element_type=F32)


def _split3(x):
    hi = x.astype(BF16)
    r = x - hi.astype(F32)
    mid = r.astype(BF16)
    lo = (r - mid.astype(F32)).astype(BF16)
    return hi, mid, lo


def _even_pre_kernel(x_ref, g_ref, wuv_ref, wq_ref, wk_ref, wv_ref, wf_ref, bf_ref,
                     qg_ref, kg_ref, qc_ref, vc_ref, vgain_ref, wsp_ref, bsp_ref, e_ref,
                     ya_ref, qt_ref, k_ref, vt_ref, carry_ref, *, tm, d_model, a_width):
    i = pl.program_id(0)

    @pl.when(i == 0)
    def _():
        carry_ref[...] = jnp.zeros_like(carry_ref)

    h = (_rms(x_ref[...], d_model) * g_ref[...]).astype(BF16)

    zuv = _dot(h, wuv_ref[...])
    u = jax.nn.gelu(zuv[:, :a_width])
    v = jax.nn.gelu(zuv[:, a_width:])
    tri = (lax.broadcasted_iota(jnp.int32, (A_CHUNK, A_CHUNK), 1)
           <= lax.broadcasted_iota(jnp.int32, (A_CHUNK, A_CHUNK), 0))
    for g in range(A_GROUPS):
        cols = slice(g * LANES, (g + 1) * LANES)
        vg = (_rms(v[:, cols], LANES) * vgain_ref[:, cols]).astype(BF16)
        w = jnp.where(tri, wsp_ref[g], 0.0).astype(BF16)
        for c in range(tm // A_CHUNK):
            rows = slice(c * A_CHUNK, (c + 1) * A_CHUNK)
            sv = _dot(w, vg[rows]) + bsp_ref[g]
            ya_ref[rows, cols] = (u[rows, cols] * sv).astype(BF16)

    f = _dot(h, wf_ref[...]) + bf_ref[...]
    ls = jnp.minimum(f, 0.0) - jnp.log1p(jnp.exp(-jnp.abs(f)))
    tri_t = (lax.broadcasted_iota(jnp.int32, (tm, tm), 1)
             <= lax.broadcasted_iota(jnp.int32, (tm, tm), 0)).astype(BF16)
    hi, mid, lo = _split3(ls)
    cum = _dot(tri_t, hi) + _dot(tri_t, mid) + _dot(tri_t, lo) + carry_ref[0:1, :]
    carry_ref[...] = jnp.broadcast_to(cum[tm - 1:tm, :], carry_ref.shape)
    chi, cmid, clo = _split3(cum * LOG2E)
    kadd = _dot(chi, e_ref[0]) + _dot(cmid, e_ref[1]) + _dot(clo, e_ref[2])

    zq = _dot(h, wq_ref[...])
    zk = _dot(h, wk_ref[...])
    zv = _dot(h, wv_ref[...])
    for hd in range(N_HEADS):
        cols = slice(hd * LANES, (hd + 1) * LANES)
        qn = _rms(zq[:, cols], HEAD_DIM) * qg_ref[...] + qc_ref[...]
        qt_ref[hd, 0] = qn.T.astype(BF16)
        k_ref[hd] = (_rms(zk[:, cols], HEAD_DIM) * kg_ref[...] + kadd[:, cols]).astype(BF16)
        vt_ref[hd, 0] = (zv[:, cols] + vc_ref[...]).T[:V_ROWS].astype(BF16)


def _prep_even(ev_w_in, ev_b_fgate, ev_q_norm, ev_k_norm, ev_v_norm, ev_b_spatial, ev_w_out):
    n, d, _ = ev_w_in.shape
    a_width = A_GROUPS * LANES
    b_width = N_HEADS * HEAD_DIM
    o = 2 * a_width
    w_in = ev_w_in.astype(BF16)
    wuv = w_in[:, :, :o]
    qkv = w_in[:, :, o:o + 3 * b_width].reshape(n, d, 3, N_HEADS, HEAD_DIM).transpose(0, 2, 1, 3, 4)
    qkv = jnp.pad(qkv, ((0, 0),) * 4 + ((0, LANES - HEAD_DIM),))
    wqkv = qkv.reshape(n, 3, d, N_HEADS * LANES)
    wf = jnp.pad(w_in[:, :, o + 3 * b_width:], ((0, 0), (0, 0), (0, LANES - N_HEADS)))
    bf = jnp.pad(ev_b_fgate, ((0, 0), (0, LANES - N_HEADS))).reshape(n, 1, LANES)
    pad = ((0, 0), (0, LANES - HEAD_DIM))
    qg = jnp.pad(ev_q_norm * (HEAD_DIM ** -0.5 * LOG2E), pad).reshape(n, 1, LANES)
    kg = jnp.pad(ev_k_norm, pad).reshape(n, 1, LANES)
    bsp = jnp.broadcast_to(ev_b_spatial[..., None], ev_b_spatial.shape + (LANES,))
    return dict(wuv=wuv, wqkv=wqkv, wf=wf, bf=bf, qg=qg, kg=kg,
                vgain=ev_v_norm.reshape(n, 1, a_width), bsp=bsp, wo=ev_w_out.astype(BF16))


def _even_consts():
    lane = np.arange(LANES)
    qc = np.where((lane >= HEAD_DIM) & (lane < HEAD_DIM + 3), -1.0, 0.0).astype(np.float32).reshape(1, LANES)
    vc = (lane == HEAD_DIM).astype(np.float32).reshape(1, LANES)
    e = np.zeros((3, LANES, N_HEADS * LANES), np.float32)
    hh = np.arange(N_HEADS)
    for part in range(3):
        e[part, hh, hh * LANES + HEAD_DIM + part] = 1.0
    return jnp.asarray(qc), jnp.asarray(vc), jnp.asarray(e, dtype=BF16)


def _even_pre(x, g_all, layer, ev, e_idx, w_sp_all):
    s, d = x.shape
    a_width = A_GROUPS * LANES
    tm = TILE_ATT
    qc, vc, e = _even_consts()
    n3 = lambda j: pl.BlockSpec((None, None, d, N_HEADS * LANES), lambda i: (e_idx, j, 0, 0),
                                pipeline_mode=pl.Buffered(1))
    row = lambda i: (i, 0)
    hrow = lambda i: (0, i, 0)
    trow = lambda i: (0, i, 0, 0)
    qsplit = TQ_ATT // tm
    kern = functools.partial(_even_pre_kernel, tm=tm, d_model=d, a_width=a_width)
    return pl.pallas_call(
        kern,
        grid=(s // tm,),
        in_specs=[
            pl.BlockSpec((tm, d), row),
            _layer_spec(g_all.shape, layer),
            _layer_spec(ev["wuv"].shape, e_idx), n3(0), n3(1), n3(2),
            _layer_spec(ev["wf"].shape, e_idx), _layer_spec(ev["bf"].shape, e_idx),
            _layer_spec(ev["qg"].shape, e_idx), _layer_spec(ev["kg"].shape, e_idx),
            _const_spec(qc.shape), _const_spec(vc.shape),
            _layer_spec(ev["vgain"].shape, e_idx), _layer_spec(w_sp_all.shape, e_idx),
            _layer_spec(ev["bsp"].shape, e_idx), _const_spec(e.shape),
        ],
        out_specs=[
            pl.BlockSpec((tm, a_width), row),
            pl.BlockSpec((N_HEADS, 1, LANES, tm), lambda i: (0, i // qsplit, 0, i % qsplit)),
            pl.BlockSpec((N_HEADS, tm, LANES), hrow),
            pl.BlockSpec((N_HEADS, 1, V_ROWS, tm), trow),
        ],
        out_shape=[jax.ShapeDtypeStruct((s, a_width), BF16),
                   jax.ShapeDtypeStruct((N_HEADS, s // TQ_ATT, LANES, TQ_ATT), BF16),
                   jax.ShapeDtypeStruct((N_HEADS, s, LANES), BF16),
                   jax.ShapeDtypeStruct((N_HEADS, s // tm, V_ROWS, tm), BF16)],
        scratch_shapes=[pltpu.VMEM((SUBLANES, LANES), F32)],
        compiler_params=pltpu.CompilerParams(
            dimension_semantics=("arbitrary",), vmem_limit_bytes=VMEM_LIMIT),
        name="even_pre",
    )(x, g_all, ev["wuv"], ev["wqkv"], ev["wqkv"], ev["wqkv"], ev["wf"], ev["bf"], ev["qg"], ev["kg"],
      qc, vc, ev["vgain"], w_sp_all, ev["bsp"], e)


def _attention_kernel(qt_ref, k_ref, vt_ref, o_ref, s0_ref, s1_ref, mx0_ref, mx1_ref, *, tq, tk):
    qi = pl.program_id(1)
    krow = lax.broadcasted_iota(jnp.int32, (SUB_K, tq), 0)
    qcol = lax.broadcasted_iota(jnp.int32, (SUB_K, tq), 1)
    n_sub = tk // SUB_K

    pieces = [(sub, hh) for sub in range(n_sub) for hh in range(2)]

    def scores(buf, kv, sub, hh):
        s_ref, mx_ref = buf
        start = pl.multiple_of(kv * tk, tk)
        kc = k_ref[hh, pl.ds(start + sub * SUB_K, SUB_K), :]
        st = _dot(kc, qt_ref[hh, 0])
        s_ref[hh, sub * SUB_K:(sub + 1) * SUB_K, :] = st
        mx_ref[sub * 2 + hh] = jnp.max(st, axis=0, keepdims=True)

    def rescale(buf, state, hh):
        _, mx_ref = buf
        m, acc = state[hh]
        m_new = m
        for sub in range(n_sub):
            m_new = jnp.maximum(m_new, mx_ref[sub * 2 + hh])
        state[hh] = (m_new, acc * jnp.exp2(m - m_new))

    def softmax_pv(buf, kv, state, diag, sub, hh):
        s_ref, _ = buf
        m, acc = state[hh]
        st = s_ref[hh, sub * SUB_K:(sub + 1) * SUB_K, :]
        if diag is not None:
            st = jnp.where(krow + (diag * tk + sub * SUB_K) <= qcol, st, NEG_INF)
            m_new = jnp.maximum(m, jnp.max(st, axis=0, keepdims=True))
            acc = acc * jnp.exp2(m - m_new)
            m = m_new
        pt = jnp.exp2(st - m).astype(BF16)
        vt = vt_ref[hh, kv][:, sub * SUB_K:(sub + 1) * SUB_K]
        state[hh] = (m, acc + _dot(vt, pt))

    def stage(state, cur, cur_kv, diag, nxt=None, nxt_kv=None):
        state = list(state)
        if diag is None:
            for hh in range(2):
                rescale(cur, state, hh)
        for sub, hh in pieces:
            if nxt is not None:
                scores(nxt, nxt_kv, sub, hh)
            softmax_pv(cur, cur_kv, state, diag, sub, hh)
        return tuple(state)

    buf0, buf1 = (s0_ref, mx0_ref), (s1_ref, mx1_ref)

    def pair(jj, state):
        c0 = 2 * jj
        state = stage(state, buf0, c0, None, buf1, c0 + 1)
        return stage(state, buf1, c0 + 1, None, buf0, c0 + 2)

    init = tuple((jnp.full((1, tq), NEG_INF, F32), jnp.zeros((V_ROWS, tq), F32)) for _ in range(2))
    for sub, hh in pieces:
        scores(buf0, 0, sub, hh)
    state = lax.fori_loop(0, qi, pair, init)
    state = stage(state, buf0, 2 * qi, 0, buf1, 2 * qi + 1)
    state = stage(state, buf1, 2 * qi + 1, 1)
    ot = jnp.concatenate([acc[:HEAD_DIM] / acc[HEAD_DIM:HEAD_DIM + 1] for _, acc in state], axis=0)
    o_ref[...] = ot.T.astype(o_ref.dtype)


def _attention(qt, k, vt):
    _, s, _ = k.shape
    tq, tk = TQ_ATT, TILE_ATT
    assert tq == 2 * tk
    nk = s // tk
    kern = functools.partial(_attention_kernel, tq=tq, tk=tk)
    return pl.pallas_call(
        kern,
        grid=(N_HEADS // 2, s // tq),
        in_specs=[
            pl.BlockSpec((2, 1, LANES, tq), lambda hp, qi: (hp, qi, 0, 0)),
            pl.BlockSpec((2, s, LANES), lambda hp, qi: (hp, 0, 0)),
            pl.BlockSpec((2, nk, V_ROWS, tk), lambda hp, qi: (hp, 0, 0, 0)),
        ],
        out_specs=pl.BlockSpec((tq, 2 * HEAD_DIM), lambda hp, qi: (qi, hp)),
        out_shape=jax.ShapeDtypeStruct((s, N_HEADS * HEAD_DIM), BF16),
        scratch_shapes=[pltpu.VMEM((2, tk, tq), F32), pltpu.VMEM((2, tk, tq), F32),
                        pltpu.VMEM((2 * tk // SUB_K, 1, tq), F32),
                        pltpu.VMEM((2 * tk // SUB_K, 1, tq), F32)],
        compiler_params=pltpu.CompilerParams(
            dimension_semantics=("arbitrary", "arbitrary"), vmem_limit_bytes=VMEM_LIMIT),
        name="attention",
    )(qt, k, vt)


def _ffn_ple_kernel(*refs, tm, d_model, n_chunks, with_mix):
    if with_mix:
        (x_ref, ya_ref, yb_ref, woa_ref, wob_ref, g_ref, wup_ref, cv_ref, wdn_ref, p_ref,
         wproj_ref, wgate_ref, o_ref, halo_ref, acc_ref, xn_ref, h0_ref, h1_ref) = refs
    else:
        (x_ref, g_ref, wup_ref, cv_ref, wdn_ref, p_ref,
         wproj_ref, wgate_ref, o_ref, halo_ref, acc_ref, xn_ref, h0_ref, h1_ref) = refs
    i = pl.program_id(0)

    @pl.when(i == 0)
    def _():
        halo_ref[...] = jnp.zeros_like(halo_ref)

    x1 = x_ref[...]
    if with_mix:
        x1 = x1 + _dot(ya_ref[...], woa_ref[...]) + _dot(yb_ref[...], wob_ref[...])
    xn_ref[...] = (_rms(x1, d_model) * g_ref[...]).astype(BF16)
    acc_ref[...] = x1

    fc = FF_CHUNK

    def up_proj(h_ref, c):
        h_ref[SUBLANES:, :] = _dot(xn_ref[...], wup_ref[c])

    def process(h_ref, c):
        h_ref[:SUBLANES, :] = halo_ref[c]
        halo_ref[c] = h_ref[tm:, :]
        cv = cv_ref[c]
        hc = (cv[3:4] + cv[0:1] * h_ref[SUBLANES - 2:tm + SUBLANES - 2, :]
              + cv[1:2] * h_ref[SUBLANES - 1:tm + SUBLANES - 1, :] + cv[2:3] * h_ref[SUBLANES:, :])
        half_gate, up = hc[:, :fc], hc[:, fc:]
        act = ((half_gate + half_gate * jnp.tanh(half_gate)) * up).astype(BF16)
        acc_ref[...] += _dot(act, wdn_ref[c])

    def pair(jj, carry):
        c0 = 2 * jj
        up_proj(h1_ref, c0 + 1)
        process(h0_ref, c0)
        up_proj(h0_ref, c0 + 2)
        process(h1_ref, c0 + 1)
        return carry

    n_pairs = (n_chunks - 1) // 2
    up_proj(h0_ref, 0)
    lax.fori_loop(0, n_pairs, pair, 0)
    if n_chunks - 2 * n_pairs == 2:
        up_proj(h1_ref, n_chunks - 1)
        process(h0_ref, n_chunks - 2)
        process(h1_ref, n_chunks - 1)
    else:
        process(h0_ref, n_chunks - 1)

    x2 = acc_ref[...]
    gate = jax.nn.sigmoid(_dot(_rms(x2, d_model).astype(BF16), wgate_ref[...]))
    o_ref[...] = x2 + gate * _dot(p_ref[...].astype(BF16), wproj_ref[...])


def _prep_ffn(ffn_w_up, ffn_conv_w, ffn_conv_b, ffn_w_down, ple_w_proj, ple_w_gate):
    n, d, two_ff = ffn_w_up.shape
    d_ff = two_ff // 2
    fc = FF_CHUNK
    nc = d_ff // fc
    wup = ffn_w_up.astype(BF16).reshape(n, d, 2, nc, fc).transpose(0, 3, 1, 2, 4).reshape(n, nc, d, 2 * fc)
    cv = jnp.concatenate([ffn_conv_w, ffn_conv_b[:, None]], axis=1)
    cv = cv.reshape(n, CONV_W + 1, 2, nc, fc) * jnp.array([0.5, 1.0], F32).reshape(1, 1, 2, 1, 1)
    cv = cv.transpose(0, 3, 1, 2, 4).reshape(n, nc, CONV_W + 1, 2 * fc)
    cv = jnp.pad(cv, ((0, 0), (0, 0), (0, SUBLANES - CONV_W - 1), (0, 0)))
    return dict(wup=wup, cv=cv, wdn=ffn_w_down.reshape(n, nc, fc, d).astype(BF16),
                wproj=ple_w_proj.astype(BF16), wgate=ple_w_gate.astype(BF16))


def _ffn_ple(x, g_all, layer, ff, p_all, batch, mix=None):
    s, d = x.shape
    _, nc, _, two_fc = ff["wup"].shape
    fc = two_fc // 2
    tm = TM_FFN
    pdim = p_all.shape[-1]

    row = lambda i: (i, 0)
    args, specs = [x], [pl.BlockSpec((tm, d), row)]
    if mix is not None:
        ya, yb, wo_all, e_idx = mix
        aw = ya.shape[1]
        half = lambda j: pl.BlockSpec((None, aw, d), lambda i: (e_idx, j, 0), pipeline_mode=pl.Buffered(1))
        args += [ya, yb, wo_all, wo_all]
        specs += [pl.BlockSpec((tm, aw), row), pl.BlockSpec((tm, yb.shape[1]), row), half(0), half(1)]
    args += [g_all, ff["wup"], ff["cv"], ff["wdn"], p_all, ff["wproj"], ff["wgate"]]
    specs += [_layer_spec(g_all.shape, layer), _layer_spec(ff["wup"].shape, layer),
              _layer_spec(ff["cv"].shape, layer), _layer_spec(ff["wdn"].shape, layer),
              pl.BlockSpec((None, None, tm, pdim), lambda i: (layer, batch, i, 0)),
              _layer_spec(ff["wproj"].shape, layer), _layer_spec(ff["wgate"].shape, layer)]
    kern = functools.partial(_ffn_ple_kernel, tm=tm, d_model=d, n_chunks=nc,
                             with_mix=mix is not None)
    return pl.pallas_call(
        kern,
        grid=(s // tm,),
        in_specs=specs,
        out_specs=pl.BlockSpec((tm, d), row),
        out_shape=jax.ShapeDtypeStruct((s, d), F32),
        scratch_shapes=[pltpu.VMEM((nc, SUBLANES, 2 * fc), F32),
                        pltpu.VMEM((tm, d), F32),
                        pltpu.VMEM((tm, d), BF16),
                        pltpu.VMEM((tm + SUBLANES, 2 * fc), F32),
                        pltpu.VMEM((tm + SUBLANES, 2 * fc), F32)],
        compiler_params=pltpu.CompilerParams(
            dimension_semantics=("arbitrary",), vmem_limit_bytes=VMEM_LIMIT),
        name="ffn_ple_mix" if mix is not None else "ffn_ple",
    )(*args)


def _s5_prep_kernel(are_ref, aim_ref, ldt_ref, bre_ref, bim_ref,
                    abre_ref, abim_ref, bbre_ref, bbim_ref):
    lr, li = are_ref[...], aim_ref[...]
    dt = jnp.exp(ldt_ref[...])
    mag = jnp.exp(lr * dt)
    ab_re = mag * jnp.cos(li * dt)
    ab_im = mag * jnp.sin(li * dt)
    den = lr * lr + li * li
    nr, ni = ab_re - 1.0, ab_im
    cr = (nr * lr + ni * li) / den
    ci = (ni * lr - nr * li) / den
    br, bi = bre_ref[...], bim_ref[...]
    abre_ref[...] = ab_re
    abim_ref[...] = ab_im
    bbre_ref[...] = cr * br - ci * bi
    bbim_ref[...] = cr * bi + ci * br


def _prep_odd(od_a_re, od_a_im, od_log_dt, od_b_re, od_b_im, od_c_re, od_c_im, od_w_glu):
    n, g, pst, c = od_b_re.shape
    shape2 = (n * g * pst * c // LANES, LANES)
    full = (n, g, pst, c)
    rep = lambda a: jnp.broadcast_to(a[..., None], full).reshape(shape2)
    ldt = jnp.broadcast_to(od_log_dt[:, :, None, None], full).reshape(shape2)
    out = jax.ShapeDtypeStruct(shape2, F32)
    ab_re, ab_im, bb_re, bb_im = pl.pallas_call(
        _s5_prep_kernel, out_shape=[out] * 4, name="s5_prep",
    )(rep(od_a_re), rep(od_a_im), ldt, od_b_re.reshape(shape2), od_b_im.reshape(shape2))
    ab_re = ab_re.reshape(full)[..., 0]
    ab_im = ab_im.reshape(full)[..., 0]

    gps = LANES // S5_GROUP_CH
    n_slab = g // gps
    slab_states = gps * pst
    eye = jnp.eye(gps, dtype=BF16)

    def bmat(bb):
        bb = bb.astype(BF16).reshape(n, n_slab, gps, pst, c)
        w = jnp.einsum('njgpc,gh->njgchp', bb, eye)
        return w.reshape(n, n_slab, LANES, slab_states)

    def cmat(cc):
        cc = cc.astype(BF16).reshape(n, n_slab, gps, c, pst)
        w = jnp.einsum('njgcp,gh->njgphc', cc, eye)
        return w.reshape(n, n_slab, slab_states, LANES)

    a = jnp.stack([ab_re.reshape(n, n_slab, slab_states), ab_im.reshape(n, n_slab, slab_states)], axis=1)
    return dict(a=a, wbre=bmat(bb_re.reshape(full)), wbim=bmat(bb_im.reshape(full)),
                cre=cmat(od_c_re), cim=cmat(od_c_im), wglu=od_w_glu.astype(BF16))


def _s5_kernel(x_ref, g_ref, wbre_ref, wbim_ref, a_ref, cre_ref, cim_ref, d_ref, wglu_ref,
               o_ref, bre_scr, bim_scr, st_scr, *, t_tile, pitch, d_model, n_slab, n_lb):
    i = pl.program_id(0)

    @pl.when(i == 0)
    def _():
        st_scr[...] = jnp.zeros_like(st_scr)

    xt = x_ref[...]
    h = _rms(xt, d_model) * g_ref[...]
    rows8 = lax.broadcasted_iota(jnp.int32, (SUBLANES, LANES), 0)

    for j in range(n_slab):
        base = j * pitch
        sh = base % SUBLANES
        hj = h[:, j * LANES:(j + 1) * LANES]
        if sh:
            hj = pltpu.roll(hj, sh, 0)
        lhs = hj.astype(BF16)
        for w_ref, scr in ((wbre_ref, bre_scr), (wbim_ref, bim_scr)):
            bu = _dot(lhs, w_ref[j])
            for k in range(n_lb):
                blk = bu[:, k * LANES:(k + 1) * LANES]
                scr[k, pl.ds(base - sh, t_tile), :] = blk
                if sh:
                    scr[k, pl.ds(base - sh + t_tile, SUBLANES), :] = blk[:SUBLANES]

    ar = [a_ref[0, :, k * LANES:(k + 1) * LANES] for k in range(n_lb)]
    ai = [a_ref[1, :, k * LANES:(k + 1) * LANES] for k in range(n_lb)]

    def body(t, carry):
        xr, xi = carry
        nxr, nxi = [], []
        for k in range(n_lb):
            idx = (k, pl.ds(t, SUBLANES, stride=pitch), slice(None))
            nr = ar[k] * xr[k] - ai[k] * xi[k] + bre_scr[idx]
            ni = ar[k] * xi[k] + ai[k] * xr[k] + bim_scr[idx]
            bre_scr[idx] = nr
            bim_scr[idx] = ni
            nxr.append(nr)
            nxi.append(ni)
        return tuple(nxr), tuple(nxi)

    init = (tuple(st_scr[0, k] for k in range(n_lb)), tuple(st_scr[1, k] for k in range(n_lb)))
    xr, xi = lax.fori_loop(0, t_tile, body, init, unroll=SCAN_UNROLL)
    for k in range(n_lb):
        st_scr[0, k] = xr[k]
        st_scr[1, k] = xi[k]

    def slab_states(scr, base, sh):
        blocks = []
        for k in range(n_lb):
            main = scr[k, pl.ds(base - sh, t_tile), :]
            if sh:
                tail = scr[k, pl.ds(base - sh + t_tile, SUBLANES), :]
                first = jnp.where(rows8 < sh, tail, main[:SUBLANES])
                main = jnp.concatenate([first, main[SUBLANES:]], axis=0)
            blocks.append(main)
        return jnp.concatenate(blocks, axis=-1).astype(BF16)

    ys = []
    for j in range(n_slab):
        base = j * pitch
        sh = base % SUBLANES
        yj = (_dot(slab_states(bre_scr, base, sh), cre_ref[j])
              - _dot(slab_states(bim_scr, base, sh), cim_ref[j]))
        if sh:
            yj = pltpu.roll(yj, t_tile - sh, 0)
        ys.append(yj)
    y = jnp.concatenate(ys, axis=-1) + d_ref[...] * h
    gl = _dot(jax.nn.gelu(y).astype(BF16), wglu_ref[...])
    o_ref[...] = xt + gl[:, :d_model] * jax.nn.sigmoid(gl[:, d_model:])


def _s5_layer(x, g_all, layer, od, o_idx, d_all):
    s, d = x.shape
    _, n_slab, _, slab_states = od["wbre"].shape
    assert n_slab == SUBLANES
    n_lb = slab_states // LANES
    t_tile, pitch = T_S5, S5_PITCH

    row = lambda i: (i, 0)
    kern = functools.partial(_s5_kernel, t_tile=t_tile, pitch=pitch, d_model=d,
                             n_slab=n_slab, n_lb=n_lb)
    return pl.pallas_call(
        kern,
        grid=(s // t_tile,),
        in_specs=[
            pl.BlockSpec((t_tile, d), row),
            _layer_spec(g_all.shape, layer),
            _layer_spec(od["wbre"].shape, o_idx), _layer_spec(od["wbim"].shape, o_idx),
            _layer_spec(od["a"].shape, o_idx),
            _layer_spec(od["cre"].shape, o_idx), _layer_spec(od["cim"].shape, o_idx),
            _layer_spec(d_all.shape, o_idx), _layer_spec(od["wglu"].shape, o_idx),
        ],
        out_specs=pl.BlockSpec((t_tile, d), row),
        out_shape=jax.ShapeDtypeStruct((s, d), F32),
        scratch_shapes=[pltpu.VMEM((n_lb, n_slab * pitch, LANES), F32),
                        pltpu.VMEM((n_lb, n_slab * pitch, LANES), F32),
                        pltpu.VMEM((2, n_lb, SUBLANES, LANES), F32)],
        compiler_params=pltpu.CompilerParams(
            dimension_semantics=("arbitrary",), vmem_limit_bytes=VMEM_LIMIT),
        name="s5_layer",
    )(x, g_all, od["wbre"], od["wbim"], od["a"], od["cre"], od["cim"], d_all, od["wglu"])


def kernel(x, p, norm_mix, norm_ffn, ev_w_in, ev_b_fgate, ev_q_norm, ev_k_norm, ev_v_norm,
           ev_w_spatial, ev_b_spatial, ev_w_out, od_a_re, od_a_im, od_log_dt, od_b_re, od_b_im,
           od_c_re, od_c_im, od_d, od_w_glu, ffn_w_up, ffn_conv_w, ffn_conv_b, ffn_w_down,
           ple_w_proj, ple_w_gate):
    bsz, seq, d = x.shape
    depth = p.shape[0]
    ev = _prep_even(ev_w_in, ev_b_fgate, ev_q_norm, ev_k_norm, ev_v_norm, ev_b_spatial, ev_w_out)
    od = _prep_odd(od_a_re, od_a_im, od_log_dt, od_b_re, od_b_im, od_c_re, od_c_im, od_w_glu)
    ff = _prep_ffn(ffn_w_up, ffn_conv_w, ffn_conv_b, ffn_w_down, ple_w_proj, ple_w_gate)
    g_mix = norm_mix.reshape(depth, 1, d)
    g_ffn = norm_ffn.reshape(depth, 1, d)
    d_skip = od_d.reshape(-1, 1, d)
    outs = []
    for b in range(bsz):
        xb = x[b]
        for i in range(depth):
            mix = None
            if i % 2 == 0:
                e = i // 2
                ya, qt, k, vt = _even_pre(xb, g_mix, i, ev, e, ev_w_spatial)
                mix = (ya, _attention(qt, k, vt), ev["wo"], e)
            else:
                xb = _s5_layer(xb, g_mix, i, od, i // 2, d_skip)
            xb = _ffn_ple(xb, g_ffn, i, ff, p, b, mix=mix)
        outs.append(xb)
    return outs[0][None] if bsz == 1 else jnp.stack(outs)
```

```python
import functools

import jax
import jax.numpy as jnp
import numpy as np
from jax import lax
from jax.experimental import pallas as pl
from jax.experimental.pallas import tpu as pltpu

F32 = jnp.float32
BF16 = jnp.bfloat16
EPS = 1e-6
NEG_INF = -1e30
LANES = 128
SUBLANES = 8
VMEM_LIMIT = 56 * 1024 * 1024

A_GROUPS = 4
A_CHUNK = 128
N_HEADS = 8
HEAD_DIM = 64
S5_GROUP_CH = 16
S5_STATE = 64
CONV_W = 3

LOG2E = 1.4426950408889634
V_ROWS = 80

TILE_ATT = 512
TQ_ATT = 1024
SUB_K = 256
TM_FFN = 512
FF_CHUNK = 256
T_S5 = 512
S5_PITCH = T_S5 + 4
SCAN_UNROLL = 4


def _const_spec(shape):
    n = len(shape)
    return pl.BlockSpec(shape, lambda *_: (0,) * n, pipeline_mode=pl.Buffered(1))


def _layer_spec(shape, layer):
    n = len(shape) - 1
    return pl.BlockSpec((None,) + tuple(shape[1:]), lambda *_: (layer,) + (0,) * n,
                        pipeline_mode=pl.Buffered(1))


def _rms(x, n):
    return x * lax.rsqrt(jnp.sum(x * x, axis=-1, keepdims=True) * (1.0 / n) + EPS)


def _dot(a, b):
    return jnp.dot(a, b, preferred_element_type=F32)


def _split3(x):
    hi = x.astype(BF16)
    r = x - hi.astype(F32)
    mid = r.astype(BF16)
    lo = (r - mid.astype(F32)).astype(BF16)
    return hi, mid, lo


def _even_pre_kernel(x_ref, g_ref, wuv_ref, wq_ref, wk_ref, wv_ref, wf_ref, bf_ref,
                     qg_ref, kg_ref, qc_ref, vc_ref, vgain_ref, wsp_ref, bsp_ref, e_ref,
                     ya_ref, qt_ref, k_ref, vt_ref, carry_ref, *, tm, d_model, a_width):
    i = pl.program_id(0)

    @pl.when(i == 0)
    def _():
        carry_ref[...] = jnp.zeros_like(carry_ref)

    h = (_rms(x_ref[...], d_model) * g_ref[...]).astype(BF16)

    zuv = _dot(h, wuv_ref[...])
    u = jax.nn.gelu(zuv[:, :a_width])
    v = jax.nn.gelu(zuv[:, a_width:])
    tri = (lax.broadcasted_iota(jnp.int32, (A_CHUNK, A_CHUNK), 1)
           <= lax.broadcasted_iota(jnp.int32, (A_CHUNK, A_CHUNK), 0))
    for g in range(A_GROUPS):
        cols = slice(g * LANES, (g + 1) * LANES)
        vg = (_rms(v[:, cols], LANES) * vgain_ref[:, cols]).astype(BF16)
        w = jnp.where(tri, wsp_ref[g], 0.0).astype(BF16)
        for c in range(tm // A_CHUNK):
            rows = slice(c * A_CHUNK, (c + 1) * A_CHUNK)
            sv = _dot(w, vg[rows]) + bsp_ref[g]
            ya_ref[rows, cols] = (u[rows, cols] * sv).astype(BF16)

    f = _dot(h, wf_ref[...]) + bf_ref[...]
    ls = jnp.minimum(f, 0.0) - jnp.log1p(jnp.exp(-jnp.abs(f)))
    tri_t = (lax.broadcasted_iota(jnp.int32, (tm, tm), 1)
             <= lax.broadcasted_iota(jnp.int32, (tm, tm), 0)).astype(BF16)
    hi, mid, lo = _split3(ls)
    cum = _dot(tri_t, hi) + _dot(tri_t, mid) + _dot(tri_t, lo) + carry_ref[0:1, :]
    carry_ref[...] = jnp.broadcast_to(cum[tm - 1:tm, :], carry_ref.shape)
    chi, cmid, clo = _split3(cum * LOG2E)
    kadd = _dot(chi, e_ref[0]) + _dot(cmid, e_ref[1]) + _dot(clo, e_ref[2])

    zq = _dot(h, wq_ref[...])
    zk = _dot(h, wk_ref[...])
    zv = _dot(h, wv_ref[...])
    for hd in range(N_HEADS):
        cols = slice(hd * LANES, (hd + 1) * LANES)
        qn = _rms(zq[:, cols], HEAD_DIM) * qg_ref[...] + qc_ref[...]
        qt_ref[hd, 0] = qn.T.astype(BF16)
        k_ref[hd] = (_rms(zk[:, cols], HEAD_DIM) * kg_ref[...] + kadd[:, cols]).astype(BF16)
        vt_ref[hd, 0] = (zv[:, cols] + vc_ref[...]).T[:V_ROWS].astype(BF16)


def _prep_even(ev_w_in, ev_b_fgate, ev_q_norm, ev_k_norm, ev_v_norm, ev_b_spatial, ev_w_out):
    n, d, _ = ev_w_in.shape
    a_width = A_GROUPS * LANES
    b_width = N_HEADS * HEAD_DIM
    o = 2 * a_width
    w_in = ev_w_in.astype(BF16)
    wuv = w_in[:, :, :o]
    qkv = w_in[:, :, o:o + 3 * b_width].reshape(n, d, 3, N_HEADS, HEAD_DIM).transpose(0, 2, 1, 3, 4)
    qkv = jnp.pad(qkv, ((0, 0),) * 4 + ((0, LANES - HEAD_DIM),))
    wqkv = qkv.reshape(n, 3, d, N_HEADS * LANES)
    wf = jnp.pad(w_in[:, :, o + 3 * b_width:], ((0, 0), (0, 0), (0, LANES - N_HEADS)))
    bf = jnp.pad(ev_b_fgate, ((0, 0), (0, LANES - N_HEADS))).reshape(n, 1, LANES)
    pad = ((0, 0), (0, LANES - HEAD_DIM))
    qg = jnp.pad(ev_q_norm * (HEAD_DIM ** -0.5 * LOG2E), pad).reshape(n, 1, LANES)
    kg = jnp.pad(ev_k_norm, pad).reshape(n, 1, LANES)
    bsp = jnp.broadcast_to(ev_b_spatial[..., None], ev_b_spatial.shape + (LANES,))
    return dict(wuv=wuv, wqkv=wqkv, wf=wf, bf=bf, qg=qg, kg=kg,
                vgain=ev_v_norm.reshape(n, 1, a_width), bsp=bsp, wo=ev_w_out.astype(BF16))


def _even_consts():
    lane = np.arange(LANES)
    qc = np.where((lane >= HEAD_DIM) & (lane < HEAD_DIM + 3), -1.0, 0.0).astype(np.float32).reshape(1, LANES)
    vc = (lane == HEAD_DIM).astype(np.float32).reshape(1, LANES)
    e = np.zeros((3, LANES, N_HEADS * LANES), np.float32)
    hh = np.arange(N_HEADS)
    for part in range(3):
        e[part, hh, hh * LANES + HEAD_DIM + part] = 1.0
    return jnp.asarray(qc), jnp.asarray(vc), jnp.asarray(e, dtype=BF16)


def _even_pre(x, g_all, layer, ev, e_idx, w_sp_all):
    s, d = x.shape
    a_width = A_GROUPS * LANES
    tm = TILE_ATT
    qc, vc, e = _even_consts()
    n3 = lambda j: pl.BlockSpec((None, None, d, N_HEADS * LANES), lambda i: (e_idx, j, 0, 0),
                                pipeline_mode=pl.Buffered(1))
    row = lambda i: (i, 0)
    hrow = lambda i: (0, i, 0)
    trow = lambda i: (0, i, 0, 0)
    qsplit = TQ_ATT // tm
    kern = functools.partial(_even_pre_kernel, tm=tm, d_model=d, a_width=a_width)
    return pl.pallas_call(
        kern,
        grid=(s // tm,),
        in_specs=[
            pl.BlockSpec((tm, d), row),
            _layer_spec(g_all.shape, layer),
            _layer_spec(ev["wuv"].shape, e_idx), n3(0), n3(1), n3(2),
            _layer_spec(ev["wf"].shape, e_idx), _layer_spec(ev["bf"].shape, e_idx),
            _layer_spec(ev["qg"].shape, e_idx), _layer_spec(ev["kg"].shape, e_idx),
            _const_spec(qc.shape), _const_spec(vc.shape),
            _layer_spec(ev["vgain"].shape, e_idx), _layer_spec(w_sp_all.shape, e_idx),
            _layer_spec(ev["bsp"].shape, e_idx), _const_spec(e.shape),
        ],
        out_specs=[
            pl.BlockSpec((tm, a_width), row),
            pl.BlockSpec((N_HEADS, 1, LANES, tm), lambda i: (0, i // qsplit, 0, i % qsplit)),
            pl.BlockSpec((N_HEADS, tm, LANES), hrow),
            pl.BlockSpec((N_HEADS, 1, V_ROWS, tm), trow),
        ],
        out_shape=[jax.ShapeDtypeStruct((s, a_width), BF16),
                   jax.ShapeDtypeStruct((N_HEADS, s // TQ_ATT, LANES, TQ_ATT), BF16),
                   jax.ShapeDtypeStruct((N_HEADS, s, LANES), BF16),
                   jax.ShapeDtypeStruct((N_HEADS, s // tm, V_ROWS, tm), BF16)],
        scratch_shapes=[pltpu.VMEM((SUBLANES, LANES), F32)],
        compiler_params=pltpu.CompilerParams(
            dimension_semantics=("arbitrary",), vmem_limit_bytes=VMEM_LIMIT),
        name="even_pre",
    )(x, g_all, ev["wuv"], ev["wqkv"], ev["wqkv"], ev["wqkv"], ev["wf"], ev["bf"], ev["qg"], ev["kg"],
      qc, vc, ev["vgain"], w_sp_all, ev["bsp"], e)


def _attention_kernel(qt_ref, k_ref, vt_ref, o_ref, s0_ref, s1_ref, mx0_ref, mx1_ref, *, tq, tk):
    qi = pl.program_id(1)
    krow = lax.broadcasted_iota(jnp.int32, (SUB_K, tq), 0)
    qcol = lax.broadcasted_iota(jnp.int32, (SUB_K, tq), 1)
    n_sub = tk // SUB_K

    pieces = [(sub, hh) for sub in range(n_sub) for hh in range(2)]

    def scores(buf, kv, sub, hh):
        s_ref, mx_ref = buf
        start = pl.multiple_of(kv * tk, tk)
        kc = k_ref[hh, pl.ds(start + sub * SUB_K, SUB_K), :]
        st = _dot(kc, qt_ref[hh, 0])
        s_ref[hh, sub * SUB_K:(sub + 1) * SUB_K, :] = st
        mx_ref[sub * 2 + hh] = jnp.max(st, axis=0, keepdims=True)

    def rescale(buf, state, hh):
        _, mx_ref = buf
        m, acc = state[hh]
        m_new = m
        for sub in range(n_sub):
            m_new = jnp.maximum(m_new, mx_ref[sub * 2 + hh])
        state[hh] = (m_new, acc * jnp.exp2(m - m_new))

    def softmax_pv(buf, kv, state, diag, sub, hh):
        s_ref, _ = buf
        m, acc = state[hh]
        st = s_ref[hh, sub * SUB_K:(sub + 1) * SUB_K, :]
        if diag is not None:
            st = jnp.where(krow + (diag * tk + sub * SUB_K) <= qcol, st, NEG_INF)
            m_new = jnp.maximum(m, jnp.max(st, axis=0, keepdims=True))
            acc = acc * jnp.exp2(m - m_new)
            m = m_new
        pt = jnp.exp2(st - m).astype(BF16)
        vt = vt_ref[hh, kv][:, sub * SUB_K:(sub + 1) * SUB_K]
        state[hh] = (m, acc + _dot(vt, pt))

    def stage(state, cur, cur_kv, diag, nxt=None, nxt_kv=None):
        state = list(state)
        if diag is None:
            for hh in range(2):
                rescale(cur, state, hh)
        for sub, hh in pieces:
            if nxt is not None:
                scores(nxt, nxt_kv, sub, hh)
            softmax_pv(cur, cur_kv, state, diag, sub, hh)
        return tuple(state)

    buf0, buf1 = (s0_ref, mx0_ref), (s1_ref, mx1_ref)

    def pair(jj, state):
        c0 = 2 * jj
        state = stage(state, buf0, c0, None, buf1, c0 + 1)
        return stage(state, buf1, c0 + 1, None, buf0, c0 + 2)

    init = tuple((jnp.full((1, tq), NEG_INF, F32), jnp.zeros((V_ROWS, tq), F32)) for _ in range(2))
    for sub, hh in pieces:
        scores(buf0, 0, sub, hh)
    state = lax.fori_loop(0, qi, pair, init)
    state = stage(state, buf0, 2 * qi, 0, buf1, 2 * qi + 1)
    state = stage(state, buf1, 2 * qi + 1, 1)
    ot = jnp.concatenate([acc[:HEAD_DIM] / acc[HEAD_DIM:HEAD_DIM + 1] for _, acc in state], axis=0)
    o_ref[...] = ot.T.astype(o_ref.dtype)


def _attention(qt, k, vt):
    _, s, _ = k.shape
    tq, tk = TQ_ATT, TILE_ATT
    assert tq == 2 * tk
    nk = s // tk
    kern = functools.partial(_attention_kernel, tq=tq, tk=tk)
    return pl.pallas_call(
        kern,
        grid=(N_HEADS // 2, s // tq),
        in_specs=[
            pl.BlockSpec((2, 1, LANES, tq), lambda hp, qi: (hp, qi, 0, 0)),
            pl.BlockSpec((2, s, LANES), lambda hp, qi: (hp, 0, 0)),
            pl.BlockSpec((2, nk, V_ROWS, tk), lambda hp, qi: (hp, 0, 0, 0)),
        ],
        out_specs=pl.BlockSpec((tq, 2 * HEAD_DIM), lambda hp, qi: (qi, hp)),
        out_shape=jax.ShapeDtypeStruct((s, N_HEADS * HEAD_DIM), BF16),
        scratch_shapes=[pltpu.VMEM((2, tk, tq), F32), pltpu.VMEM((2, tk, tq), F32),
                        pltpu.VMEM((2 * tk // SUB_K, 1, tq), F32),
                        pltpu.VMEM((2 * tk // SUB_K, 1, tq), F32)],
        compiler_params=pltpu.CompilerParams(
            dimension_semantics=("arbitrary", "arbitrary"), vmem_limit_bytes=VMEM_LIMIT),
        name="attention",
    )(qt, k, vt)


def _ffn_ple_kernel(*refs, tm, d_model, n_chunks, with_mix):
    if with_mix:
        (x_ref, ya_ref, yb_ref, woa_ref, wob_ref, g_ref, wup_ref, cv_ref, wdn_ref, p_ref,
         wproj_ref, wgate_ref, o_ref, halo_ref, acc_ref, xn_ref, h0_ref, h1_ref) = refs
    else:
        (x_ref, g_ref, wup_ref, cv_ref, wdn_ref, p_ref,
         wproj_ref, wgate_ref, o_ref, halo_ref, acc_ref, xn_ref, h0_ref, h1_ref) = refs
    i = pl.program_id(0)

    @pl.when(i == 0)
    def _():
        halo_ref[...] = jnp.zeros_like(halo_ref)

    x1 = x_ref[...]
    if with_mix:
        x1 = x1 + _dot(ya_ref[...], woa_ref[...]) + _dot(yb_ref[...], wob_ref[...])
    xn_ref[...] = (_rms(x1, d_model) * g_ref[...]).astype(BF16)
    acc_ref[...] = x1

    fc = FF_CHUNK

    d_ff = n_chunks * fc

    def up_proj(h_ref, c):
        col = pl.multiple_of(c * fc, fc)
        xn = xn_ref[...]
        h_ref[SUBLANES:, :fc] = _dot(xn, wup_ref[:, pl.ds(col, fc)])
        h_ref[SUBLANES:, fc:] = _dot(xn, wup_ref[:, pl.ds(d_ff + col, fc)])

    def process(h_ref, c):
        h_ref[:SUBLANES, :] = halo_ref[c]
        halo_ref[c] = h_ref[tm:, :]
        cv = cv_ref[c]
        hc = (cv[3:4] + cv[0:1] * h_ref[SUBLANES - 2:tm + SUBLANES - 2, :]
              + cv[1:2] * h_ref[SUBLANES - 1:tm + SUBLANES - 1, :] + cv[2:3] * h_ref[SUBLANES:, :])
        half_gate, up = hc[:, :fc], hc[:, fc:]
        act = ((half_gate + half_gate * jnp.tanh(half_gate)) * up).astype(BF16)
        acc_ref[...] += _dot(act, wdn_ref[c])

    def pair(jj, carry):
        c0 = 2 * jj
        up_proj(h1_ref, c0 + 1)
        process(h0_ref, c0)
        up_proj(h0_ref, c0 + 2)
        process(h1_ref, c0 + 1)
        return carry

    n_pairs = (n_chunks - 1) // 2
    up_proj(h0_ref, 0)
    lax.fori_loop(0, n_pairs, pair, 0)
    if n_chunks - 2 * n_pairs == 2:
        up_proj(h1_ref, n_chunks - 1)
        process(h0_ref, n_chunks - 2)
        process(h1_ref, n_chunks - 1)
    else:
        process(h0_ref, n_chunks - 1)

    x2 = acc_ref[...]
    gate = jax.nn.sigmoid(_dot(_rms(x2, d_model).astype(BF16), wgate_ref[...]))
    o_ref[...] = x2 + gate * _dot(p_ref[...].astype(BF16), wproj_ref[...])


def _prep_ffn(ffn_w_up, ffn_conv_w, ffn_conv_b, ffn_w_down, ple_w_proj, ple_w_gate):
    n, d, two_ff = ffn_w_up.shape
    d_ff = two_ff // 2
    fc = FF_CHUNK
    nc = d_ff // fc
    wup = ffn_w_up.astype(BF16)
    cv =jnp.concatenate([ffn_conv_w, ffn_conv_b[:, None]], axis=1)
    cv = cv.reshape(n, CONV_W + 1, 2, nc, fc) * jnp.array([0.5, 1.0], F32).reshape(1, 1, 2, 1, 1)
    cv = cv.transpose(0, 3, 1, 2, 4).reshape(n, nc, CONV_W + 1, 2 * fc)
    cv = jnp.pad(cv, ((0, 0), (0, 0), (0, SUBLANES - CONV_W - 1), (0, 0)))
    return dict(wup=wup, cv=cv, wdn=ffn_w_down.reshape(n, nc, fc, d).astype(BF16),
                wproj=ple_w_proj.astype(BF16), wgate=ple_w_gate.astype(BF16))


def _ffn_ple(x, g_all, layer, ff, p_all, batch, mix=None):
    s, d = x.shape
    _, nc, fc, _ = ff["wdn"].shape
    tm = TM_FFN
    pdim = p_all.shape[-1]

    row = lambda i: (i, 0)
    args, specs = [x], [pl.BlockSpec((tm, d), row)]
    if mix is not None:
        ya, yb, wo_all, e_idx = mix
        aw = ya.shape[1]
        half = lambda j: pl.BlockSpec((None, aw, d), lambda i: (e_idx, j, 0), pipeline_mode=pl.Buffered(1))
        args += [ya, yb, wo_all, wo_all]
        specs += [pl.BlockSpec((tm, aw), row), pl.BlockSpec((tm, yb.shape[1]), row), half(0), half(1)]
    args += [g_all, ff["wup"], ff["cv"], ff["wdn"], p_all, ff["wproj"], ff["wgate"]]
    specs += [_layer_spec(g_all.shape, layer), _layer_spec(ff["wup"].shape, layer),
              _layer_spec(ff["cv"].shape, layer), _layer_spec(ff["wdn"].shape, layer),
              pl.BlockSpec((None, None, tm, pdim), lambda i: (layer, batch, i, 0)),
              _layer_spec(ff["wproj"].shape, layer), _layer_spec(ff["wgate"].shape, layer)]
    kern = functools.partial(_ffn_ple_kernel, tm=tm, d_model=d, n_chunks=nc,
                             with_mix=mix is not None)
    return pl.pallas_call(
        kern,
        grid=(s // tm,),
        in_specs=specs,
        out_specs=pl.BlockSpec((tm, d), row),
        out_shape=jax.ShapeDtypeStruct((s, d), F32),
        scratch_shapes=[pltpu.VMEM((nc, SUBLANES, 2 * fc), F32),
                        pltpu.VMEM((tm, d), F32),
                        pltpu.VMEM((tm, d), BF16),
                        pltpu.VMEM((tm + SUBLANES, 2 * fc), F32),
                        pltpu.VMEM((tm + SUBLANES, 2 * fc), F32)],
        compiler_params=pltpu.CompilerParams(
            dimension_semantics=("arbitrary",), vmem_limit_bytes=VMEM_LIMIT),
        name="ffn_ple_mix" if mix is not None else "ffn_ple",
    )(*args)


def _s5_prep_kernel(are_ref, aim_ref, ldt_ref, bre_ref, bim_ref,
                    abre_ref, abim_ref, bbre_ref, bbim_ref):
    lr, li = are_ref[...], aim_ref[...]
    dt = jnp.exp(ldt_ref[...])
    mag = jnp.exp(lr * dt)
    ab_re = mag * jnp.cos(li * dt)
    ab_im = mag * jnp.sin(li * dt)
    den = lr * lr + li * li
    nr, ni = ab_re - 1.0, ab_im
    cr = (nr * lr + ni * li) / den
    ci = (ni * lr - nr * li) / den
    br, bi = bre_ref[...], bim_ref[...]
    abre_ref[...] = ab_re
    abim_ref[...] = ab_im
    bbre_ref[...] = cr * br - ci * bi
    bbim_ref[...] = cr * bi + ci * br


def _prep_odd(od_a_re, od_a_im, od_log_dt, od_b_re, od_b_im, od_c_re, od_c_im, od_w_glu):
    n, g, pst, c = od_b_re.shape
    shape2 = (n * g * pst * c // LANES, LANES)
    full = (n, g, pst, c)
    rep = lambda a: jnp.broadcast_to(a[..., None], full).reshape(shape2)
    ldt = jnp.broadcast_to(od_log_dt[:, :, None, None], full).reshape(shape2)
    out = jax.ShapeDtypeStruct(shape2, F32)
    ab_re, ab_im, bb_re, bb_im = pl.pallas_call(
        _s5_prep_kernel, out_shape=[out] * 4, name="s5_prep",
    )(rep(od_a_re), rep(od_a_im), ldt, od_b_re.reshape(shape2), od_b_im.reshape(shape2))
    ab_re = ab_re.reshape(full)[..., 0]
    ab_im = ab_im.reshape(full)[..., 0]

    gps = LANES // S5_GROUP_CH
    n_slab = g // gps
    slab_states = gps * pst
    eye = jnp.eye(gps, dtype=BF16)

    def bmat(bb):
        bb = bb.astype(BF16).reshape(n, n_slab, gps, pst, c)
        w = jnp.einsum('njgpc,gh->njgchp', bb, eye)
        return w.reshape(n, n_slab, LANES, slab_states)

    def cmat(cc):
        cc = cc.astype(BF16).reshape(n, n_slab, gps, c, pst)
        w = jnp.einsum('njgcp,gh->njgphc', cc, eye)
        return w.reshape(n, n_slab, slab_states, LANES)

    a = jnp.stack([ab_re.reshape(n, n_slab, slab_states), ab_im.reshape(n, n_slab, slab_states)], axis=1)
    return dict(a=a, wbre=bmat(bb_re.reshape(full)), wbim=bmat(bb_im.reshape(full)),
                cre=cmat(od_c_re), cim=cmat(od_c_im), wglu=od_w_glu.astype(BF16))


def _s5_kernel(x_ref, g_ref, wbre_ref, wbim_ref, a_ref, cre_ref, cim_ref, d_ref, wglu_ref,
               o_ref, bre_scr, bim_scr, st_scr, *, t_tile, pitch, d_model, n_slab, n_lb):
    i = pl.program_id(0)

    @pl.when(i == 0)
    def _():
        st_scr[...] = jnp.zeros_like(st_scr)

    xt = x_ref[...]
    h = _rms(xt, d_model) * g_ref[...]
    rows8 = lax.broadcasted_iota(jnp.int32, (SUBLANES, LANES), 0)

    for j in range(n_slab):
        base = j * pitch
        sh = base % SUBLANES
        hj = h[:, j * LANES:(j + 1) * LANES]
        if sh:
            hj = pltpu.roll(hj, sh, 0)
        lhs = hj.astype(BF16)
        for w_ref, scr in ((wbre_ref, bre_scr), (wbim_ref, bim_scr)):
            bu = _dot(lhs, w_ref[j])
            for k in range(n_lb):
                blk = bu[:, k * LANES:(k + 1) * LANES]
                scr[k, pl.ds(base - sh, t_tile), :] = blk
                if sh:
                    scr[k, pl.ds(base - sh + t_tile, SUBLANES), :] = blk[:SUBLANES]

    ar = [a_ref[0, :, k * LANES:(k + 1) * LANES] for k in range(n_lb)]
    ai = [a_ref[1, :, k * LANES:(k + 1) * LANES] for k in range(n_lb)]

    def body(t, carry):
        xr, xi = carry
        nxr, nxi = [], []
        for k in range(n_lb):
            idx = (k, pl.ds(t, SUBLANES, stride=pitch), slice(None))
            nr = ar[k] * xr[k] - ai[k] * xi[k] + bre_scr[idx]
            ni = ar[k] * xi[k] + ai[k] * xr[k] + bim_scr[idx]
            bre_scr[idx] = nr
            bim_scr[idx] = ni
            nxr.append(nr)
            nxi.append(ni)
        return tuple(nxr), tuple(nxi)

    init = (tuple(st_scr[0, k] for k in range(n_lb)), tuple(st_scr[1, k] for k in range(n_lb)))
    xr, xi = lax.fori_loop(0, t_tile, body, init, unroll=SCAN_UNROLL)
    for k in range(n_lb):
        st_scr[0, k] = xr[k]
        st_scr[1, k] = xi[k]

    def slab_states(scr, base, sh):
        blocks = []
        for k in range(n_lb):
            main = scr[k, pl.ds(base - sh, t_tile), :]
            if sh:
                tail = scr[k, pl.ds(base - sh + t_tile, SUBLANES), :]
                first = jnp.where(rows8 < sh, tail, main[:SUBLANES])
                main = jnp.concatenate([first, main[SUBLANES:]], axis=0)
            blocks.append(main)
        return jnp.concatenate(blocks, axis=-1).astype(BF16)

    ys = []
    for j in range(n_slab):
        base = j * pitch
        sh = base % SUBLANES
        yj = (_dot(slab_states(bre_scr, base, sh), cre_ref[j])
              - _dot(slab_states(bim_scr, base, sh), cim_ref[j]))
        if sh:
            yj = pltpu.roll(yj, t_tile - sh, 0)
        ys.append(yj)
    y = jnp.concatenate(ys, axis=-1) + d_ref[...] * h
    gl = _dot(jax.nn.gelu(y).astype(BF16), wglu_ref[...])
    o_ref[...] = xt + gl[:, :d_model] * jax.nn.sigmoid(gl[:, d_model:])


def _s5_layer(x, g_all, layer, od, o_idx, d_all):
    s, d = x.shape
    _, n_slab, _, slab_states = od["wbre"].shape
    assert n_slab == SUBLANES
    n_lb = slab_states // LANES
    t_tile, pitch = T_S5, S5_PITCH

    row = lambda i: (i, 0)
    kern = functools.partial(_s5_kernel, t_tile=t_tile, pitch=pitch, d_model=d,
                             n_slab=n_slab, n_lb=n_lb)
    return pl.pallas_call(
        kern,
        grid=(s // t_tile,),
        in_specs=[
            pl.BlockSpec((t_tile, d), row),
            _layer_spec(g_all.shape, layer),
            _layer_spec(od["wbre"].shape, o_idx), _layer_spec(od["wbim"].shape, o_idx),
            _layer_spec(od["a"].shape, o_idx),
            _layer_spec(od["cre"].shape, o_idx), _layer_spec(od["cim"].shape, o_idx),
            _layer_spec(d_all.shape, o_idx), _layer_spec(od["wglu"].shape, o_idx),
        ],
        out_specs=pl.BlockSpec((t_tile, d), row),
        out_shape=jax.ShapeDtypeStruct((s, d), F32),
        scratch_shapes=[pltpu.VMEM((n_lb, n_slab * pitch, LANES), F32),
                        pltpu.VMEM((n_lb, n_slab * pitch, LANES), F32),
                        pltpu.VMEM((2, n_lb, SUBLANES, LANES), F32)],
        compiler_params=pltpu.CompilerParams(
            dimension_semantics=("arbitrary",), vmem_limit_bytes=VMEM_LIMIT),
        name="s5_layer",
    )(x, g_all, od["wbre"], od["wbim"], od["a"], od["cre"], od["cim"], d_all, od["wglu"])


def kernel(x, p, norm_mix, norm_ffn, ev_w_in, ev_b_fgate, ev_q_norm, ev_k_norm, ev_v_norm,
           ev_w_spatial, ev_b_spatial, ev_w_out, od_a_re, od_a_im, od_log_dt, od_b_re, od_b_im,
           od_c_re, od_c_im, od_d, od_w_glu, ffn_w_up, ffn_conv_w, ffn_conv_b, ffn_w_down,
           ple_w_proj, ple_w_gate):
    bsz, seq, d = x.shape
    depth = p.shape[0]
    ev = _prep_even(ev_w_in, ev_b_fgate, ev_q_norm, ev_k_norm, ev_v_norm, ev_b_spatial, ev_w_out)
    od = _prep_odd(od_a_re, od_a_im, od_log_dt, od_b_re, od_b_im, od_c_re, od_c_im, od_w_glu)
    ff = _prep_ffn(ffn_w_up, ffn_conv_w, ffn_conv_b, ffn_w_down, ple_w_proj, ple_w_gate)
    g_mix = norm_mix.reshape(depth, 1, d)
    g_ffn = norm_ffn.reshape(depth, 1, d)
    d_skip = od_d.reshape(-1, 1, d)
    outs = []
    for b in range(bsz):
        xb = x[b]
        for i in range(depth):
            mix = None
            if i % 2 == 0:
                e = i // 2
                ya, qt, k, vt = _even_pre(xb, g_mix, i, ev, e, ev_w_spatial)
                mix = (ya, _attention(qt, k, vt), ev["wo"], e)
            else:
                xb = _s5_layer(xb, g_mix, i, od, i // 2, d_skip)
            xb = _ffn_ple(xb, g_ffn, i, ff, p, b, mix=mix)
        outs.append(xb)
    return outs[0][None] if bsz == 1 else jnp.stack(outs)
```

```python
import functools

import jax
import jax.numpy as jnp
import numpy as np
from jax import lax
from jax.experimental import pallas as pl
from jax.experimental.pallas import tpu as pltpu

F32 = jnp.float32
BF16 = jnp.bfloat16
EPS = 1e-6
NEG_INF = -1e30
LANES = 128
SUBLANES = 8
VMEM_LIMIT = 56 * 1024 * 1024

A_GROUPS = 4
A_CHUNK = 128
N_HEADS = 8
HEAD_DIM = 64
S5_GROUP_CH = 16
S5_STATE = 64
CONV_W = 3

LOG2E = 1.4426950408889634
V_ROWS = 80

TILE_ATT = 512
TQ_ATT = 1024
SUB_K = 256
TM_FFN = 512
FF_CHUNK = 256
T_S5 = 512
S5_PITCH = T_S5 + 4
SCAN_UNROLL = 4


def _const_spec(shape):
    n = len(shape)
    return pl.BlockSpec(shape, lambda *_: (0,) * n, pipeline_mode=pl.Buffered(1))


def _layer_spec(shape, layer):
    n = len(shape) - 1
    return pl.BlockSpec((None,) + tuple(shape[1:]), lambda *_: (layer,) + (0,) * n,
                        pipeline_mode=pl.Buffered(1))


def _rms(x, n):
    return x * lax.rsqrt(jnp.sum(x * x, axis=-1, keepdims=True) * (1.0 / n) + EPS)


def _dot(a, b):
    return jnp.dot(a, b, preferred_element_type=F32)


def _split3(x):
    hi = x.astype(BF16)
    r = x - hi.astype(F32)
    mid = r.astype(BF16)
    lo = (r - mid.astype(F32)).astype(BF16)
    return hi, mid, lo


def _even_pre_kernel(x_ref, g_ref, wuv_ref, wq_ref, wk_ref, wv_ref, wf_ref, bf_ref,
                     qg_ref, kg_ref, kmask_ref, qc_ref, vc_ref, vgain_ref, wsp_ref, bsp_ref, e_ref,
                     ya_ref, qt_ref, k_ref, vt_ref, carry_ref, *, tm, d_model, a_width):
    i = pl.program_id(0)

    @pl.when(i == 0)
    def _():
        carry_ref[...] = jnp.zeros_like(carry_ref)

    h = (_rms(x_ref[...], d_model) * g_ref[...]).astype(BF16)

    zuv = _dot(h, wuv_ref[...])
    u = jax.nn.gelu(zuv[:, :a_width])
    v = jax.nn.gelu(zuv[:, a_width:])
    tri = (lax.broadcasted_iota(jnp.int32, (A_CHUNK, A_CHUNK), 1)
           <= lax.broadcasted_iota(jnp.int32, (A_CHUNK, A_CHUNK), 0))
    for g in range(A_GROUPS):
        cols = slice(g * LANES, (g + 1) * LANES)
        vg = (_rms(v[:, cols], LANES) * vgain_ref[:, cols]).astype(BF16)
        w = jnp.where(tri, wsp_ref[g], 0.0).astype(BF16)
        for c in range(tm // A_CHUNK):
            rows = slice(c * A_CHUNK, (c + 1) * A_CHUNK)
            sv = _dot(w, vg[rows]) + bsp_ref[g]
            ya_ref[rows, cols] = (u[rows, cols] * sv).astype(BF16)

    f = _dot(h, wf_ref[...]) + bf_ref[...]
    ls = jnp.minimum(f, 0.0) - jnp.log1p(jnp.exp(-jnp.abs(f)))
    tri_t = (lax.broadcasted_iota(jnp.int32, (tm, tm), 1)
             <= lax.broadcasted_iota(jnp.int32, (tm, tm), 0)).astype(BF16)
    hi, mid, lo = _split3(ls)
    cum = _dot(tri_t, hi) + _dot(tri_t, mid) + _dot(tri_t, lo) + carry_ref[0:1, :]
    carry_ref[...] = jnp.broadcast_to(cum[tm - 1:tm, :], carry_ref.shape)
    chi, cmid, clo = _split3(cum * LOG2E)
    kadd = _dot(chi, e_ref[0]) + _dot(cmid, e_ref[1]) + _dot(clo, e_ref[2])

    zqt = _dot(h, wq_ref[...]).T
    zvt = _dot(h, wv_ref[...]).T
    zk = _dot(h, wk_ref[...])
    for hd in range(N_HEADS):
        rows = slice(hd * HEAD_DIM, (hd + 1) * HEAD_DIM)
        qh = zqt[rows]
        qn = qh * lax.rsqrt(jnp.sum(qh * qh, axis=0, keepdims=True) * (1.0 / HEAD_DIM) + EPS)
        qt_ref[hd, 0] = jnp.concatenate([qn * qg_ref[...], qc_ref[...]], axis=0).astype(BF16)
        vt_ref[hd, 0] = jnp.concatenate([zvt[rows], vc_ref[...]], axis=0).astype(BF16)
        kb = zk[:, (hd // 2) * LANES:(hd // 2 + 1) * LANES]
        if hd % 2:
            kb = pltpu.roll(kb, HEAD_DIM, 1)
        kb = kb * kmask_ref[...]
        k_ref[hd] = (_rms(kb, HEAD_DIM) * kg_ref[...] + kadd[:, hd * LANES:(hd + 1) * LANES]).astype(BF16)


def _prep_even(ev_w_in, ev_b_fgate, ev_q_norm, ev_k_norm, ev_v_norm, ev_b_spatial, ev_w_out):
    n, d, _ = ev_w_in.shape
    a_width = A_GROUPS * LANES
    b_width = N_HEADS * HEAD_DIM
    o = 2 * a_width
    w_in = ev_w_in.astype(BF16)
    wf = jnp.pad(w_in[:, :, o + 3 * b_width:], ((0, 0), (0, 0), (0, LANES - N_HEADS)))
    bf = jnp.pad(ev_b_fgate, ((0, 0), (0, LANES - N_HEADS))).reshape(n, 1, LANES)
    qg = jnp.broadcast_to((ev_q_norm * (HEAD_DIM ** -0.5 * LOG2E))[:, :, None], (n, HEAD_DIM, TILE_ATT))
    kg = jnp.pad(ev_k_norm, ((0, 0), (0, LANES - HEAD_DIM))).reshape(n, 1, LANES)
    bsp = jnp.broadcast_to(ev_b_spatial[..., None], ev_b_spatial.shape + (LANES,))
    return dict(w_in=w_in, wf=wf, bf=bf, qg=qg, kg=kg,
                vgain=ev_v_norm.reshape(n, 1, a_width), bsp=bsp, wo=ev_w_out.astype(BF16))


def _even_consts():
    row = np.arange(LANES - HEAD_DIM)[:, None]
    qc = np.broadcast_to(np.where(row < 3, -1.0, 0.0), (LANES - HEAD_DIM, TILE_ATT)).astype(np.float32)
    vc = np.broadcast_to(np.where(row[:V_ROWS - HEAD_DIM] == 0, 1.0, 0.0),
                         (V_ROWS - HEAD_DIM, TILE_ATT)).astype(np.float32)
    kmask = (np.arange(LANES) < HEAD_DIM).astype(np.float32).reshape(1, LANES)
    e = np.zeros((3, LANES, N_HEADS * LANES), np.float32)
    hh = np.arange(N_HEADS)
    for part in range(3):
        e[part, hh, hh * LANES + HEAD_DIM + part] = 1.0
    return jnp.asarray(qc), jnp.asarray(vc), jnp.asarray(kmask), jnp.asarray(e, dtype=BF16)


def _even_pre(x, g_all, layer, ev, e_idx, w_sp_all):
    s, d = x.shape
    a_width = A_GROUPS * LANES
    b_width = N_HEADS * HEAD_DIM
    tm = TILE_ATT
    qc, vc, kmask, e = _even_consts()
    assert (2 * a_width) % b_width == 0
    q_blk = 2 * a_width // b_width
    wcols = lambda width, j: pl.BlockSpec((None, d, width), lambda i: (e_idx, 0, j),
                                          pipeline_mode=pl.Buffered(1))
    row = lambda i: (i, 0)
    hrow = lambda i: (0, i, 0)
    trow = lambda i: (0, i, 0, 0)
    qsplit = TQ_ATT // tm
    kern = functools.partial(_even_pre_kernel, tm=tm, d_model=d, a_width=a_width)
    return pl.pallas_call(
        kern,
        grid=(s // tm,),
        in_specs=[
            pl.BlockSpec((tm, d), row),
            _layer_spec(g_all.shape, layer),
            wcols(2 * a_width, 0), wcols(b_width, q_blk), wcols(b_width, q_blk + 1),
            wcols(b_width, q_blk + 2),
            _layer_spec(ev["wf"].shape, e_idx), _layer_spec(ev["bf"].shape, e_idx),
            _layer_spec(ev["qg"].shape, e_idx), _layer_spec(ev["kg"].shape, e_idx),
            _const_spec(kmask.shape), _const_spec(qc.shape), _const_spec(vc.shape),
            _layer_spec(ev["vgain"].shape, e_idx), _layer_spec(w_sp_all.shape, e_idx),
            _layer_spec(ev["bsp"].shape, e_idx), _const_spec(e.shape),
        ],
        out_specs=[
            pl.BlockSpec((tm, a_width), row),
            pl.BlockSpec((N_HEADS, 1, LANES, tm), lambda i: (0, i // qsplit, 0, i % qsplit)),
            pl.BlockSpec((N_HEADS, tm, LANES), hrow),
            pl.BlockSpec((N_HEADS, 1, V_ROWS, tm), trow),
        ],
        out_shape=[jax.ShapeDtypeStruct((s, a_width), BF16),
                   jax.ShapeDtypeStruct((N_HEADS, s // TQ_ATT, LANES, TQ_ATT), BF16),
                   jax.ShapeDtypeStruct((N_HEADS, s, LANES), BF16),
                   jax.ShapeDtypeStruct((N_HEADS, s // tm, V_ROWS, tm), BF16)],
        scratch_shapes=[pltpu.VMEM((SUBLANES, LANES), F32)],
        compiler_params=pltpu.CompilerParams(
            dimension_semantics=("arbitrary",), vmem_limit_bytes=VMEM_LIMIT),
        name="even_pre",
    )(x, g_all, ev["w_in"], ev["w_in"], ev["w_in"], ev["w_in"], ev["wf"], ev["bf"], ev["qg"], ev["kg"],
      kmask, qc, vc, ev["vgain"], w_sp_all, ev["bsp"], e)


def _attention_kernel(qt_ref, k_ref, vt_ref, o_ref, s0_ref, s1_ref, mx0_ref, mx1_ref, *, tq, tk):
    qi = pl.program_id(1)
    krow = lax.broadcasted_iota(jnp.int32, (SUB_K, tq), 0)
    qcol = lax.broadcasted_iota(jnp.int32, (SUB_K, tq), 1)
    n_sub = tk // SUB_K

    pieces = [(sub, hh) for sub in range(n_sub) for hh in range(2)]

    def scores(buf, kv, sub, hh, q0=0):
        s_ref, mx_ref = buf
        start = pl.multiple_of(kv * tk, tk)
        kc = k_ref[hh, pl.ds(start + sub * SUB_K, SUB_K), :]
        if q0:
            s_ref[hh, sub * SUB_K:(sub + 1) * SUB_K, q0:] = _dot(kc, qt_ref[hh, 0][:, q0:])
            return
        st = _dot(kc, qt_ref[hh, 0])
        s_ref[hh, sub * SUB_K:(sub + 1) * SUB_K, :] = st
        mx_ref[sub * 2 + hh] = jnp.max(st, axis=0, keepdims=True)

    def rescale(buf, state, hh):
        _, mx_ref = buf
        m, acc = state[hh]
        m_new = m
        for sub in range(n_sub):
            m_new = jnp.maximum(m_new, mx_ref[sub * 2 + hh])
        state[hh] = (m_new, acc * jnp.exp2(m - m_new))

    def softmax_pv(buf, kv, state, diag, sub, hh):
        s_ref, _ = buf
        m_all, acc_all = state[hh]
        q0 = 0 if diag is None else diag * tk
        m, acc = m_all[:, q0:], acc_all[:, q0:]
        st = s_ref[hh, sub * SUB_K:(sub + 1) * SUB_K, q0:]
        if diag is not None:
            kpos = lax.broadcasted_iota(jnp.int32, st.shape, 0) + (q0 + sub * SUB_K)
            qpos = lax.broadcasted_iota(jnp.int32, st.shape, 1) + q0
            st = jnp.where(kpos <= qpos, st, NEG_INF)
            m_new = jnp.maximum(m, jnp.max(st, axis=0, keepdims=True))
            acc = acc * jnp.exp2(m - m_new)
            m = m_new
        pt = jnp.exp2(st - m).astype(BF16)
        vt = vt_ref[hh, kv][:, sub * SUB_K:(sub + 1) * SUB_K]
        acc = acc + _dot(vt, pt)
        if q0:
            m = jnp.concatenate([m_all[:, :q0], m], axis=1)
            acc = jnp.concatenate([acc_all[:, :q0], acc], axis=1)
        state[hh] = (m, acc)

    def stage(state, cur, cur_kv, diag, nxt=None, nxt_kv=None, nxt_q0=0):
        state = list(state)
        if diag is None:
            for hh in range(2):
                rescale(cur, state, hh)
        for sub, hh in pieces:
            if nxt is not None:
                scores(nxt, nxt_kv, sub, hh, nxt_q0)
            softmax_pv(cur, cur_kv, state, diag, sub, hh)
        return tuple(state)

    buf0, buf1 = (s0_ref, mx0_ref), (s1_ref, mx1_ref)

    def pair(jj, state):
        c0 = 2 * jj
        state = stage(state, buf0, c0, None, buf1, c0 + 1)
        return stage(state, buf1, c0 + 1, None, buf0, c0 + 2)

    init = tuple((jnp.full((1, tq), NEG_INF, F32), jnp.zeros((V_ROWS, tq), F32)) for _ in range(2))
    for sub, hh in pieces:
        scores(buf0, 0, sub, hh)
    state = lax.fori_loop(0, qi, pair, init)
    state = stage(state, buf0, 2 * qi, 0, buf1, 2 * qi + 1, nxt_q0=tk)
    state = stage(state, buf1, 2 * qi + 1, 1)
    ot = jnp.concatenate([acc[:HEAD_DIM] / acc[HEAD_DIM:HEAD_DIM + 1] for _, acc in state], axis=0)
    o_ref[...] = ot.T.astype(o_ref.dtype)


def _attention(qt, k, vt):
    _, s, _ = k.shape
    tq, tk = TQ_ATT, TILE_ATT
    assert tq == 2 * tk
    nk = s // tk
    kern = functools.partial(_attention_kernel, tq=tq, tk=tk)
    return pl.pallas_call(
        kern,
        grid=(N_HEADS // 2, s // tq),
        in_specs=[
            pl.BlockSpec((2, 1, LANES, tq), lambda hp, qi: (hp, qi, 0, 0)),
            pl.BlockSpec((2, s, LANES), lambda hp, qi: (hp, 0, 0)),
            pl.BlockSpec((2, nk, V_ROWS, tk), lambda hp, qi: (hp, 0, 0, 0)),
        ],
        out_specs=pl.BlockSpec((tq, 2 * HEAD_DIM), lambda hp, qi: (qi, hp)),
        out_shape=jax.ShapeDtypeStruct((s, N_HEADS * HEAD_DIM), BF16),
        scratch_shapes=[pltpu.VMEM((2, tk, tq), F32), pltpu.VMEM((2, tk, tq), F32),
                        pltpu.VMEM((2 * tk // SUB_K, 1, tq), F32),
                        pltpu.VMEM((2 * tk // SUB_K, 1, tq), F32)],
        compiler_params=pltpu.CompilerParams(
            dimension_semantics=("arbitrary", "arbitrary"), vmem_limit_bytes=VMEM_LIMIT),
        name="attention",
    )(qt, k, vt)


def _ffn_ple_kernel(*refs, tm, d_model, n_chunks, with_mix):
    if with_mix:
        (x_ref, ya_ref, yb_ref, woa_ref, wob_ref, g_ref, wup_ref, cv_ref, wdn_ref, p_ref,
         wproj_ref, wgate_ref, o_ref, halo_ref, acc_ref, xn_ref, h0_ref, h1_ref) = refs
    else:
        (x_ref, g_ref, wup_ref, cv_ref, wdn_ref, p_ref,
         wproj_ref, wgate_ref, o_ref, halo_ref, acc_ref, xn_ref, h0_ref, h1_ref) = refs
    i = pl.program_id(0)

    @pl.when(i == 0)
    def _():
        halo_ref[...] = jnp.zeros_like(halo_ref)

    x1 = x_ref[...]
    if with_mix:
        x1 = x1 + _dot(ya_ref[...], woa_ref[...]) + _dot(yb_ref[...], wob_ref[...])
    xn_ref[...] = (_rms(x1, d_model) * g_ref[...]).astype(BF16)
    acc_ref[...] = x1

    fc = FF_CHUNK

    d_ff = n_chunks * fc

    def up_proj(h_ref, c):
        col = pl.multiple_of(c * fc, fc)
        xn = xn_ref[...]
        h_ref[SUBLANES:, :fc] = _dot(xn, wup_ref[:, pl.ds(col, fc)])
        h_ref[SUBLANES:, fc:] = _dot(xn, wup_ref[:, pl.ds(d_ff + col, fc)])

    def process(h_ref, c):
        h_ref[:SUBLANES, :] = halo_ref[c]
        halo_ref[c] = h_ref[tm:, :]
        cv = cv_ref[c]
        hc = (cv[3:4] + cv[0:1] * h_ref[SUBLANES - 2:tm + SUBLANES - 2, :]
              + cv[1:2] * h_ref[SUBLANES - 1:tm + SUBLANES - 1, :] + cv[2:3] * h_ref[SUBLANES:, :])
        half_gate, up = hc[:, :fc], hc[:, fc:]
        act = ((half_gate + half_gate * jnp.tanh(half_gate)) * up).astype(BF16)
        acc_ref[...] += _dot(act, wdn_ref[c])

    def pair(jj, carry):
        c0 = 2 * jj
        up_proj(h1_ref, c0 + 1)
        process(h0_ref, c0)
        up_proj(h0_ref, c0 + 2)
        process(h1_ref, c0 + 1)
        return carry

    n_pairs = (n_chunks - 1) // 2
    up_proj(h0_ref, 0)
    lax.fori_loop(0, n_pairs, pair, 0)
    if n_chunks - 2 * n_pairs == 2:
        up_proj(h1_ref, n_chunks - 1)
        process(h0_ref, n_chunks - 2)
        process(h1_ref, n_chunks - 1)
    else:
        process(h0_ref, n_chunks - 1)

    x2 = acc_ref[...]
    gate = jax.nn.sigmoid(_dot(_rms(x2, d_model).astype(BF16), wgate_ref[...]))
    o_ref[...] = x2 + gate * _dot(p_ref[...].astype(BF16), wproj_ref[...])


def _prep_ffn(ffn_w_up, ffn_conv_w, ffn_conv_b, ffn_w_down, ple_w_proj, ple_w_gate):
    n, d, two_ff = ffn_w_up.shape
    d_ff = two_ff // 2
    fc = FF_CHUNK
    nc = d_ff // fc
    wup = ffn_w_up.astype(BF16)
    cv =jnp.concatenate([ffn_conv_w, ffn_conv_b[:, None]], axis=1)
    cv = cv.reshape(n, CONV_W + 1, 2, nc, fc) * jnp.array([0.5, 1.0], F32).reshape(1, 1, 2, 1, 1)
    cv = cv.transpose(0, 3, 1, 2, 4).reshape(n, nc, CONV_W + 1, 2 * fc)
    cv = jnp.pad(cv, ((0, 0), (0, 0), (0, SUBLANES - CONV_W - 1), (0, 0)))
    return dict(wup=wup, cv=cv, wdn=ffn_w_down.reshape(n, nc, fc, d).astype(BF16),
                wproj=ple_w_proj.astype(BF16), wgate=ple_w_gate.astype(BF16))


def _ffn_ple(x, g_all, layer, ff, p_all, batch, mix=None):
    s, d = x.shape
    _, nc, fc, _ = ff["wdn"].shape
    tm = TM_FFN
    pdim = p_all.shape[-1]

    row = lambda i: (i, 0)
    args, specs = [x], [pl.BlockSpec((tm, d), row)]
    if mix is not None:
        ya, yb, wo_all, e_idx = mix
        aw = ya.shape[1]
        half = lambda j: pl.BlockSpec((None, aw, d), lambda i: (e_idx, j, 0), pipeline_mode=pl.Buffered(1))
        args += [ya, yb, wo_all, wo_all]
        specs += [pl.BlockSpec((tm, aw), row), pl.BlockSpec((tm, yb.shape[1]), row), half(0), half(1)]
    args += [g_all, ff["wup"], ff["cv"], ff["wdn"], p_all, ff["wproj"], ff["wgate"]]
    specs += [_layer_spec(g_all.shape, layer), _layer_spec(ff["wup"].shape, layer),
              _layer_spec(ff["cv"].shape, layer), _layer_spec(ff["wdn"].shape, layer),
              pl.BlockSpec((None, None, tm, pdim), lambda i: (layer, batch, i, 0)),
              _layer_spec(ff["wproj"].shape, layer), _layer_spec(ff["wgate"].shape, layer)]
    kern = functools.partial(_ffn_ple_kernel, tm=tm, d_model=d, n_chunks=nc,
                             with_mix=mix is not None)
    return pl.pallas_call(
        kern,
        grid=(s // tm,),
        in_specs=specs,
        out_specs=pl.BlockSpec((tm, d), row),
        out_shape=jax.ShapeDtypeStruct((s, d), F32),
        scratch_shapes=[pltpu.VMEM((nc, SUBLANES, 2 * fc), F32),
                        pltpu.VMEM((tm, d), F32),
                        pltpu.VMEM((tm, d), BF16),
                        pltpu.VMEM((tm + SUBLANES, 2 * fc), F32),
                        pltpu.VMEM((tm + SUBLANES, 2 * fc), F32)],
        compiler_params=pltpu.CompilerParams(
            dimension_semantics=("arbitrary",), vmem_limit_bytes=VMEM_LIMIT),
        name="ffn_ple_mix" if mix is not None else "ffn_ple",
    )(*args)


def _s5_prep_kernel(are_ref, aim_ref, ldt_ref, bre_ref, bim_ref,
                    abre_ref, abim_ref, bbre_ref, bbim_ref):
    lr, li = are_ref[...], aim_ref[...]
    dt = jnp.exp(ldt_ref[...])
    mag = jnp.exp(lr * dt)
    ab_re = mag * jnp.cos(li * dt)
    ab_im = mag * jnp.sin(li * dt)
    den = lr * lr + li * li
    nr, ni = ab_re - 1.0, ab_im
    cr = (nr * lr + ni * li) / den
    ci = (ni * lr - nr * li) / den
    br, bi = bre_ref[...], bim_ref[...]
    abre_ref[...] = ab_re
    abim_ref[...] = ab_im
    bbre_ref[...] = cr * br - ci * bi
    bbim_ref[...] = cr * bi + ci * br


def _prep_odd(od_a_re, od_a_im, od_log_dt, od_b_re, od_b_im, od_c_re, od_c_im, od_w_glu):
    n, g, pst, c = od_b_re.shape
    shape2 = (n * g * pst * c // LANES, LANES)
    full = (n, g, pst, c)
    rep = lambda a: jnp.broadcast_to(a[..., None], full).reshape(shape2)
    ldt = jnp.broadcast_to(od_log_dt[:, :, None, None], full).reshape(shape2)
    out = jax.ShapeDtypeStruct(shape2, F32)
    ab_re, ab_im, bb_re, bb_im = pl.pallas_call(
        _s5_prep_kernel, out_shape=[out] * 4, name="s5_prep",
    )(rep(od_a_re), rep(od_a_im), ldt, od_b_re.reshape(shape2), od_b_im.reshape(shape2))
    ab_re = ab_re.reshape(full)[..., 0]
    ab_im = ab_im.reshape(full)[..., 0]

    gps = LANES // S5_GROUP_CH
    n_slab = g // gps
    slab_states = gps * pst
    eye = jnp.eye(gps, dtype=BF16)

    def bmat(bb):
        bb = bb.astype(BF16).reshape(n, n_slab, gps, pst, c)
        w = jnp.einsum('njgpc,gh->njgchp', bb, eye)
        return w.reshape(n, n_slab, LANES, slab_states)

    def cmat(cc):
        cc = cc.astype(BF16).reshape(n, n_slab, gps, c, pst)
        w = jnp.einsum('njgcp,gh->njgphc', cc, eye)
        return w.reshape(n, n_slab, slab_states, LANES)

    a = jnp.stack([ab_re.reshape(n, n_slab, slab_states), ab_im.reshape(n, n_slab, slab_states)], axis=1)
    return dict(a=a, wbre=bmat(bb_re.reshape(full)), wbim=bmat(bb_im.reshape(full)),
                cre=cmat(od_c_re), cim=cmat(od_c_im), wglu=od_w_glu.astype(BF16))


def _s5_kernel(x_ref, g_ref, wbre_ref, wbim_ref, a_ref, cre_ref, cim_ref, d_ref, wglu_ref,
               o_ref, bre_scr, bim_scr, st_scr, *, t_tile, pitch, d_model, n_slab, n_lb):
    i = pl.program_id(0)

    @pl.when(i == 0)
    def _():
        st_scr[...] = jnp.zeros_like(st_scr)

    xt = x_ref[...]
    h = _rms(xt, d_model) * g_ref[...]
    rows8 = lax.broadcasted_iota(jnp.int32, (SUBLANES, LANES), 0)

    for j in range(n_slab):
        base = j * pitch
        sh = base % SUBLANES
        hj = h[:, j * LANES:(j + 1) * LANES]
        if sh:
            hj = pltpu.roll(hj, sh, 0)
        lhs = hj.astype(BF16)
        for w_ref, scr in ((wbre_ref, bre_scr), (wbim_ref, bim_scr)):
            bu = _dot(lhs, w_ref[j])
            for k in range(n_lb):
                blk = bu[:, k * LANES:(k + 1) * LANES]
                scr[k, pl.ds(base - sh, t_tile), :] = blk
                if sh:
                    scr[k, pl.ds(base - sh + t_tile, SUBLANES), :] = blk[:SUBLANES]

    ar = [a_ref[0, :, k * LANES:(k + 1) * LANES] for k in range(n_lb)]
    ai = [a_ref[1, :, k * LANES:(k + 1) * LANES] for k in range(n_lb)]

    def body(t, carry):
        xr, xi = carry
        nxr, nxi = [], []
        for k in range(n_lb):
            idx = (k, pl.ds(t, SUBLANES, stride=pitch), slice(None))
            nr = ar[k] * xr[k] - ai[k] * xi[k] + bre_scr[idx]
            ni = ar[k] * xi[k] + ai[k] * xr[k] + bim_scr[idx]
            bre_scr[idx] = nr
            bim_scr[idx] = ni
            nxr.append(nr)
            nxi.append(ni)
        return tuple(nxr), tuple(nxi)

    init = (tuple(st_scr[0, k] for k in range(n_lb)), tuple(st_scr[1, k] for k in range(n_lb)))
    xr, xi = lax.fori_loop(0, t_tile, body, init, unroll=SCAN_UNROLL)
    for k in range(n_lb):
        st_scr[0, k] = xr[k]
        st_scr[1, k] = xi[k]

    def slab_states(scr, base, sh):
        blocks = []
        for k in range(n_lb):
            main = scr[k, pl.ds(base - sh, t_tile), :]
            if sh:
                tail = scr[k, pl.ds(base - sh + t_tile, SUBLANES), :]
                first = jnp.where(rows8 < sh, tail, main[:SUBLANES])
                main = jnp.concatenate([first, main[SUBLANES:]], axis=0)
            blocks.append(main)
        return jnp.concatenate(blocks, axis=-1).astype(BF16)

    ys = []
    for j in range(n_slab):
        base = j * pitch
        sh = base % SUBLANES
        yj = (_dot(slab_states(bre_scr, base, sh), cre_ref[j])
              - _dot(slab_states(bim_scr, base, sh), cim_ref[j]))
        if sh:
            yj = pltpu.roll(yj, t_tile - sh, 0)
        ys.append(yj)
    y = jnp.concatenate(ys, axis=-1) + d_ref[...] * h
    gl = _dot(jax.nn.gelu(y).astype(BF16), wglu_ref[...])
    o_ref[...] = xt + gl[:, :d_model] * jax.nn.sigmoid(gl[:, d_model:])


def _s5_layer(x, g_all, layer, od, o_idx, d_all):
    s, d = x.shape
    _, n_slab, _, slab_states = od["wbre"].shape
    assert n_slab == SUBLANES
    n_lb = slab_states // LANES
    t_tile, pitch = T_S5, S5_PITCH

    row = lambda i: (i, 0)
    kern = functools.partial(_s5_kernel, t_tile=t_tile, pitch=pitch, d_model=d,
                             n_slab=n_slab, n_lb=n_lb)
    return pl.pallas_call(
        kern,
        grid=(s // t_tile,),
        in_specs=[
            pl.BlockSpec((t_tile, d), row),
            _layer_spec(g_all.shape, layer),
            _layer_spec(od["wbre"].shape, o_idx), _layer_spec(od["wbim"].shape, o_idx),
            _layer_spec(od["a"].shape, o_idx),
            _layer_spec(od["cre"].shape, o_idx), _layer_spec(od["cim"].shape, o_idx),
            _layer_spec(d_all.shape, o_idx), _layer_spec(od["wglu"].shape, o_idx),
        ],
        out_specs=pl.BlockSpec((t_tile, d), row),
        out_shape=jax.ShapeDtypeStruct((s, d), F32),
        scratch_shapes=[pltpu.VMEM((n_lb, n_slab * pitch, LANES), F32),
                        pltpu.VMEM((n_lb, n_slab * pitch, LANES), F32),
                        pltpu.VMEM((2, n_lb, SUBLANES, LANES), F32)],
        compiler_params=pltpu.CompilerParams(
            dimension_semantics=("arbitrary",), vmem_limit_bytes=VMEM_LIMIT),
        name="s5_layer",
    )(x, g_all, od["wbre"], od["wbim"], od["a"], od["cre"], od["cim"], d_all, od["wglu"])


def kernel(x, p, norm_mix, norm_ffn, ev_w_in, ev_b_fgate, ev_q_norm, ev_k_norm, ev_v_norm,
           ev_w_spatial, ev_b_spatial, ev_w_out, od_a_re, od_a_im, od_log_dt, od_b_re, od_b_im,
           od_c_re, od_c_im, od_d, od_w_glu, ffn_w_up, ffn_conv_w, ffn_conv_b, ffn_w_down,
           ple_w_proj, ple_w_gate):
    bsz, seq, d = x.shape
    depth = p.shape[0]
    ev = _prep_even(ev_w_in, ev_b_fgate, ev_q_norm, ev_k_norm, ev_v_norm, ev_b_spatial, ev_w_out)
    od = _prep_odd(od_a_re, od_a_im, od_log_dt, od_b_re, od_b_im, od_c_re, od_c_im, od_w_glu)
    ff = _prep_ffn(ffn_w_up, ffn_conv_w, ffn_conv_b, ffn_w_down, ple_w_proj, ple_w_gate)
    g_mix = norm_mix.reshape(depth, 1, d)
    g_ffn = norm_ffn.reshape(depth, 1, d)
    d_skip = od_d.reshape(-1, 1, d)
    outs = []
    for b in range(bsz):
        xb = x[b]
        for i in range(depth):
            mix = None
            if i % 2 == 0:
                e = i // 2
                ya, qt, k, vt = _even_pre(xb, g_mix, i, ev, e, ev_w_spatial)
                mix = (ya, _attention(qt, k, vt), ev["wo"], e)
            else:
                xb = _s5_layer(xb, g_mix, i, od, i // 2, d_skip)
            xb = _ffn_ple(xb, g_ffn, i, ff, p, b, mix=mix)
        outs.append(xb)
    return outs[0][None] if bsz == 1 else jnp.stack(outs)
```

```python
import functools

import jax
import jax.numpy as jnp
import numpy as np
from jax import lax
from jax.experimental import pallas as pl
from jax.experimental.pallas import tpu as pltpu

F32 = jnp.float32
BF16 = jnp.bfloat16
EPS = 1e-6
NEG_INF = -1e30
LANES = 128
SUBLANES = 8
VMEM_LIMIT = 56 * 1024 * 1024

A_GROUPS = 4
A_CHUNK = 128
N_HEADS = 8
HEAD_DIM = 64
S5_GROUP_CH = 16
S5_STATE = 64
CONV_W = 3

LOG2E = 1.4426950408889634
V_ROWS = 80

TILE_ATT = 512
TQ_ATT = 1024
SUB_K = 256
TM_FFN = 512
FF_CHUNK = 256
T_S5 = 512
S5_PITCH = T_S5 + 4
SCAN_UNROLL = 4


def _const_spec(shape):
    n = len(shape)
    return pl.BlockSpec(shape, lambda *_: (0,) * n, pipeline_mode=pl.Buffered(1))


def _layer_spec(shape, layer):
    n = len(shape) - 1
    return pl.BlockSpec((None,) + tuple(shape[1:]), lambda *_: (layer,) + (0,) * n,
                        pipeline_mode=pl.Buffered(1))


def _rms(x, n):
    return x * lax.rsqrt(jnp.sum(x * x, axis=-1, keepdims=True) * (1.0 / n) + EPS)


def _dot(a, b):
    return jnp.dot(a, b, preferred_element_type=F32)


def _split3(x):
    hi = x.astype(BF16)
    r = x - hi.astype(F32)
    mid = r.astype(BF16)
    lo = (r - mid.astype(F32)).astype(BF16)
    return hi, mid, lo


def _even_pre_kernel(x_ref, g_ref, wuv_ref, wq_ref, wk_ref, wv_ref, wf_ref, bf_ref,
                     qg_ref, kg_ref, kmask_ref, qc_ref, vc_ref, vgain_ref, wsp_ref, bsp_ref, e_ref,
                     ya_ref, qt_ref, k_ref, vt_ref, carry_ref, *, tm, d_model, a_width):
    i = pl.program_id(0)

    @pl.when(i == 0)
    def _():
        carry_ref[...] = jnp.zeros_like(carry_ref)

    h = (_rms(x_ref[...], d_model) * g_ref[...]).astype(BF16)

    zuv = _dot(h, wuv_ref[...])
    u = jax.nn.gelu(zuv[:, :a_width])
    v = jax.nn.gelu(zuv[:, a_width:])
    tri = (lax.broadcasted_iota(jnp.int32, (A_CHUNK, A_CHUNK), 1)
           <= lax.broadcasted_iota(jnp.int32, (A_CHUNK, A_CHUNK), 0))
    for g in range(A_GROUPS):
        cols = slice(g * LANES, (g + 1) * LANES)
        vg = (_rms(v[:, cols], LANES) * vgain_ref[:, cols]).astype(BF16)
        w = jnp.where(tri, wsp_ref[g], 0.0).astype(BF16)
        n_ch = tm // A_CHUNK
        vcat = jnp.concatenate([vg[c * A_CHUNK:(c + 1) * A_CHUNK] for c in range(n_ch)], axis=1)
        sv = _dot(w, vcat)
        for c in range(n_ch):
            rows = slice(c * A_CHUNK, (c + 1) * A_CHUNK)
            ya_ref[rows, cols] = (u[rows, cols] * (sv[:, c * LANES:(c + 1) * LANES] + bsp_ref[g])).astype(BF16)

    f = _dot(h, wf_ref[...]) + bf_ref[...]
    ls = jnp.minimum(f, 0.0) - jnp.log1p(jnp.exp(-jnp.abs(f)))
    tri_t = (lax.broadcasted_iota(jnp.int32, (tm, tm), 1)
             <= lax.broadcasted_iota(jnp.int32, (tm, tm), 0)).astype(BF16)
    c3 = _dot(tri_t, jnp.concatenate(_split3(ls), axis=1))
    cum = c3[:, :LANES] + c3[:, LANES:2 * LANES] + c3[:, 2 * LANES:] + carry_ref[0:1, :]
    carry_ref[...] = jnp.broadcast_to(cum[tm - 1:tm, :], carry_ref.shape)
    kadd = _dot(jnp.concatenate(_split3(cum * LOG2E), axis=1), e_ref[...])

    zqt = _dot(h, wq_ref[...]).T
    zvt = _dot(h, wv_ref[...]).T
    zk = _dot(h, wk_ref[...])
    for hd in range(N_HEADS):
        rows = slice(hd * HEAD_DIM, (hd + 1) * HEAD_DIM)
        qh = zqt[rows]
        qn = qh * lax.rsqrt(jnp.sum(qh * qh, axis=0, keepdims=True) * (1.0 / HEAD_DIM) + EPS)
        qt_ref[hd, 0] = jnp.concatenate([qn * qg_ref[...], qc_ref[...]], axis=0).astype(BF16)
        vt_ref[hd, 0] = jnp.concatenate([zvt[rows], vc_ref[...]], axis=0).astype(BF16)
        kb = zk[:, (hd // 2) * LANES:(hd // 2 + 1) * LANES]
        if hd % 2:
            kb = pltpu.roll(kb, HEAD_DIM, 1)
        kb = kb * kmask_ref[...]
        k_ref[hd] = (_rms(kb, HEAD_DIM) * kg_ref[...] + kadd[:, hd * LANES:(hd + 1) * LANES]).astype(BF16)


def _prep_even(ev_w_in, ev_b_fgate, ev_q_norm, ev_k_norm, ev_v_norm, ev_b_spatial, ev_w_out):
    n, d, _ = ev_w_in.shape
    a_width = A_GROUPS * LANES
    b_width = N_HEADS * HEAD_DIM
    o = 2 * a_width
    w_in = ev_w_in.astype(BF16)
    wf = jnp.pad(w_in[:, :, o + 3 * b_width:], ((0, 0), (0, 0), (0, LANES - N_HEADS)))
    bf = jnp.pad(ev_b_fgate, ((0, 0), (0, LANES - N_HEADS))).reshape(n, 1, LANES)
    qg = jnp.broadcast_to((ev_q_norm * (HEAD_DIM ** -0.5 * LOG2E))[:, :, None], (n, HEAD_DIM, TILE_ATT))
    kg = jnp.pad(ev_k_norm, ((0, 0), (0, LANES - HEAD_DIM))).reshape(n, 1, LANES)
    bsp = jnp.broadcast_to(ev_b_spatial[..., None], ev_b_spatial.shape + (LANES,))
    return dict(w_in=w_in, wf=wf, bf=bf, qg=qg, kg=kg,
                vgain=ev_v_norm.reshape(n, 1, a_width), bsp=bsp, wo=ev_w_out.astype(BF16))


def _even_consts():
    row = np.arange(LANES - HEAD_DIM)[:, None]
    qc = np.broadcast_to(np.where(row < 3, -1.0, 0.0), (LANES - HEAD_DIM, TILE_ATT)).astype(np.float32)
    vc = np.broadcast_to(np.where(row[:V_ROWS - HEAD_DIM] == 0, 1.0, 0.0),
                         (V_ROWS - HEAD_DIM, TILE_ATT)).astype(np.float32)
    kmask = (np.arange(LANES) < HEAD_DIM).astype(np.float32).reshape(1, LANES)
    e = np.zeros((3 * LANES, N_HEADS * LANES), np.float32)
    hh = np.arange(N_HEADS)
    for part in range(3):
        e[part * LANES + hh, hh * LANES + HEAD_DIM + part] = 1.0
    return jnp.asarray(qc), jnp.asarray(vc), jnp.asarray(kmask), jnp.asarray(e, dtype=BF16)


def _even_pre(x, g_all, layer, ev, e_idx, w_sp_all):
    s, d = x.shape
    a_width = A_GROUPS * LANES
    b_width = N_HEADS * HEAD_DIM
    tm = TILE_ATT
    qc, vc, kmask, e = _even_consts()
    assert (2 * a_width) % b_width == 0
    q_blk = 2 * a_width // b_width
    wcols = lambda width, j: pl.BlockSpec((None, d, width), lambda i: (e_idx, 0, j),
                                          pipeline_mode=pl.Buffered(1))
    row = lambda i: (i, 0)
    hrow = lambda i: (0, i, 0)
    trow = lambda i: (0, i, 0, 0)
    qsplit = TQ_ATT // tm
    kern = functools.partial(_even_pre_kernel, tm=tm, d_model=d, a_width=a_width)
    return pl.pallas_call(
        kern,
        grid=(s // tm,),
        in_specs=[
            pl.BlockSpec((tm, d), row),
            _layer_spec(g_all.shape, layer),
            wcols(2 * a_width, 0), wcols(b_width, q_blk), wcols(b_width, q_blk + 1),
            wcols(b_width, q_blk + 2),
            _layer_spec(ev["wf"].shape, e_idx), _layer_spec(ev["bf"].shape, e_idx),
            _layer_spec(ev["qg"].shape, e_idx), _layer_spec(ev["kg"].shape, e_idx),
            _const_spec(kmask.shape), _const_spec(qc.shape), _const_spec(vc.shape),
            _layer_spec(ev["vgain"].shape, e_idx), _layer_spec(w_sp_all.shape, e_idx),
            _layer_spec(ev["bsp"].shape, e_idx), _const_spec(e.shape),
        ],
        out_specs=[
            pl.BlockSpec((tm, a_width), row),
            pl.BlockSpec((N_HEADS, 1, LANES, tm), lambda i: (0, i // qsplit, 0, i % qsplit)),
            pl.BlockSpec((N_HEADS, tm, LANES), hrow),
            pl.BlockSpec((N_HEADS, 1, V_ROWS, tm), trow),
        ],
        out_shape=[jax.ShapeDtypeStruct((s, a_width), BF16),
                   jax.ShapeDtypeStruct((N_HEADS, s // TQ_ATT, LANES, TQ_ATT), BF16),
                   jax.ShapeDtypeStruct((N_HEADS, s, LANES), BF16),
                   jax.ShapeDtypeStruct((N_HEADS, s // tm, V_ROWS, tm), BF16)],
        scratch_shapes=[pltpu.VMEM((SUBLANES, LANES), F32)],
        compiler_params=pltpu.CompilerParams(
            dimension_semantics=("arbitrary",), vmem_limit_bytes=VMEM_LIMIT),
        name="even_pre",
    )(x, g_all, ev["w_in"], ev["w_in"], ev["w_in"], ev["w_in"], ev["wf"], ev["bf"], ev["qg"], ev["kg"],
      kmask, qc, vc, ev["vgain"], w_sp_all, ev["bsp"], e)


def _attention_kernel(qt_ref, k_ref, vt_ref, o_ref, s0_ref, s1_ref, mx0_ref, mx1_ref, *, tq, tk):
    qi = pl.program_id(1)
    krow = lax.broadcasted_iota(jnp.int32, (SUB_K, tq), 0)
    qcol = lax.broadcasted_iota(jnp.int32, (SUB_K, tq), 1)
    n_sub = tk // SUB_K

    pieces = [(sub, hh) for sub in range(n_sub) for hh in range(2)]

    def scores(buf, kv, sub, hh, q0=0):
        s_ref, mx_ref = buf
        start = pl.multiple_of(kv * tk, tk)
        kc = k_ref[hh, pl.ds(start + sub * SUB_K, SUB_K), :]
        if q0:
            s_ref[hh, sub * SUB_K:(sub + 1) * SUB_K, q0:] = _dot(kc, qt_ref[hh, 0][:, q0:])
            return
        st = _dot(kc, qt_ref[hh, 0])
        s_ref[hh, sub * SUB_K:(sub + 1) * SUB_K, :] = st
        mx_ref[sub * 2 + hh] = jnp.max(st, axis=0, keepdims=True)

    def rescale(buf, state, hh):
        _, mx_ref = buf
        m, acc = state[hh]
        m_new = m
        for sub in range(n_sub):
            m_new = jnp.maximum(m_new, mx_ref[sub * 2 + hh])
        state[hh] = (m_new, acc * jnp.exp2(m - m_new))

    def softmax_pv(buf, kv, state, diag, sub, hh):
        s_ref, _ = buf
        m_all, acc_all = state[hh]
        q0 = 0 if diag is None else diag * tk
        m, acc = m_all[:, q0:], acc_all[:, q0:]
        st = s_ref[hh, sub * SUB_K:(sub + 1) * SUB_K, q0:]
        if diag is not None:
            kpos = lax.broadcasted_iota(jnp.int32, st.shape, 0) + (q0 + sub * SUB_K)
            qpos = lax.broadcasted_iota(jnp.int32, st.shape, 1) + q0
            st = jnp.where(kpos <= qpos, st, NEG_INF)
            m_new = jnp.maximum(m, jnp.max(st, axis=0, keepdims=True))
            acc = acc * jnp.exp2(m - m_new)
            m = m_new
        pt = jnp.exp2(st - m).astype(BF16)
        vt = vt_ref[hh, kv][:, sub * SUB_K:(sub + 1) * SUB_K]
        acc = acc + _dot(vt, pt)
        if q0:
            m = jnp.concatenate([m_all[:, :q0], m], axis=1)
            acc = jnp.concatenate([acc_all[:, :q0], acc], axis=1)
        state[hh] = (m, acc)

    def stage(state, cur, cur_kv, diag, nxt=None, nxt_kv=None, nxt_q0=0):
        state = list(state)
        if diag is None:
            for hh in range(2):
                rescale(cur, state, hh)
        for sub, hh in pieces:
            if nxt is not None:
                scores(nxt, nxt_kv, sub, hh, nxt_q0)
            softmax_pv(cur, cur_kv, state, diag, sub, hh)
        return tuple(state)

    buf0, buf1 = (s0_ref, mx0_ref), (s1_ref, mx1_ref)

    def pair(jj, state):
        c0 = 2 * jj
        state = stage(state, buf0, c0, None, buf1, c0 + 1)
        return stage(state, buf1, c0 + 1, None, buf0, c0 + 2)

    init = tuple((jnp.full((1, tq), NEG_INF, F32), jnp.zeros((V_ROWS, tq), F32)) for _ in range(2))
    for sub, hh in pieces:
        scores(buf0, 0, sub, hh)
    state = lax.fori_loop(0, qi, pair, init)
    state = stage(state, buf0, 2 * qi, 0, buf1, 2 * qi + 1, nxt_q0=tk)
    state = stage(state, buf1, 2 * qi + 1, 1)
    ot = jnp.concatenate([acc[:HEAD_DIM] / acc[HEAD_DIM:HEAD_DIM + 1] for _, acc in state], axis=0)
    o_ref[...] = ot.T.astype(o_ref.dtype)


def _attention(qt, k, vt):
    _, s, _ = k.shape
    tq, tk = TQ_ATT, TILE_ATT
    assert tq == 2 * tk
    nk = s // tk
    kern = functools.partial(_attention_kernel, tq=tq, tk=tk)
    return pl.pallas_call(
        kern,
        grid=(N_HEADS // 2, s // tq),
        in_specs=[
            pl.BlockSpec((2, 1, LANES, tq), lambda hp, qi: (hp, qi, 0, 0)),
            pl.BlockSpec((2, s, LANES), lambda hp, qi: (hp, 0, 0)),
            pl.BlockSpec((2, nk, V_ROWS, tk), lambda hp, qi: (hp, 0, 0, 0)),
        ],
        out_specs=pl.BlockSpec((tq, 2 * HEAD_DIM), lambda hp, qi: (qi, hp)),
        out_shape=jax.ShapeDtypeStruct((s, N_HEADS * HEAD_DIM), BF16),
        scratch_shapes=[pltpu.VMEM((2, tk, tq), F32), pltpu.VMEM((2, tk, tq), F32),
                        pltpu.VMEM((2 * tk // SUB_K, 1, tq), F32),
                        pltpu.VMEM((2 * tk // SUB_K, 1, tq), F32)],
        compiler_params=pltpu.CompilerParams(
            dimension_semantics=("arbitrary", "arbitrary"), vmem_limit_bytes=VMEM_LIMIT),
        name="attention",
    )(qt, k, vt)


def _ffn_ple_kernel(*refs, tm, d_model, n_chunks, with_mix):
    if with_mix:
        (x_ref, ya_ref, yb_ref, wo_ref, g_ref, wup_ref, cv_ref, wdn_ref, p_ref,
         wproj_ref, wgate_ref, o_ref, halo_ref, acc_ref, xn_ref, h0_ref, h1_ref) = refs
    else:
        (x_ref, g_ref, wup_ref, cv_ref, wdn_ref, p_ref,
         wproj_ref, wgate_ref, o_ref, halo_ref, acc_ref, xn_ref, h0_ref, h1_ref) = refs
    i = pl.program_id(0)

    @pl.when(i == 0)
    def _():
        halo_ref[...] = jnp.zeros_like(halo_ref)

    x1 = x_ref[...]
    if with_mix:
        x1 = x1 + _dot(jnp.concatenate([ya_ref[...], yb_ref[...]], axis=1), wo_ref[...])
    xn_ref[...] = (_rms(x1, d_model) * g_ref[...]).astype(BF16)
    acc_ref[...] = x1

    fc = FF_CHUNK

    d_ff = n_chunks * fc

    def up_proj(h_ref, c):
        col = pl.multiple_of(c * fc, fc)
        xn = xn_ref[...]
        h_ref[SUBLANES:, :fc] = _dot(xn, wup_ref[:, pl.ds(col, fc)])
        h_ref[SUBLANES:, fc:] = _dot(xn, wup_ref[:, pl.ds(d_ff + col, fc)])

    def process(h_ref, c):
        h_ref[:SUBLANES, :] = halo_ref[c]
        halo_ref[c] = h_ref[tm:, :]
        cv = cv_ref[c]
        hc = (cv[3:4] + cv[0:1] * h_ref[SUBLANES - 2:tm + SUBLANES - 2, :]
              + cv[1:2] * h_ref[SUBLANES - 1:tm + SUBLANES - 1, :] + cv[2:3] * h_ref[SUBLANES:, :])
        half_gate, up = hc[:, :fc], hc[:, fc:]
        act = ((half_gate + half_gate * jnp.tanh(half_gate)) * up).astype(BF16)
        acc_ref[...] += _dot(act, wdn_ref[c])

    def pair(jj, carry):
        c0 = 2 * jj
        up_proj(h1_ref, c0 + 1)
        process(h0_ref, c0)
        up_proj(h0_ref, c0 + 2)
        process(h1_ref, c0 + 1)
        return carry

    n_pairs = (n_chunks - 1) // 2
    up_proj(h0_ref, 0)
    lax.fori_loop(0, n_pairs, pair, 0)
    if n_chunks - 2 * n_pairs == 2:
        up_proj(h1_ref, n_chunks - 1)
        process(h0_ref, n_chunks - 2)
        process(h1_ref, n_chunks - 1)
    else:
        process(h0_ref, n_chunks - 1)

    x2 = acc_ref[...]
    gate = jax.nn.sigmoid(_dot(_rms(x2, d_model).astype(BF16), wgate_ref[...]))
    o_ref[...] = x2 + gate * _dot(p_ref[...].astype(BF16), wproj_ref[...])


def _prep_ffn(ffn_w_up, ffn_conv_w, ffn_conv_b, ffn_w_down, ple_w_proj, ple_w_gate):
    n, d, two_ff = ffn_w_up.shape
    d_ff = two_ff // 2
    fc = FF_CHUNK
    nc = d_ff // fc
    wup = ffn_w_up.astype(BF16)
    cv =jnp.concatenate([ffn_conv_w, ffn_conv_b[:, None]], axis=1)
    cv = cv.reshape(n, CONV_W + 1, 2, nc, fc) * jnp.array([0.5, 1.0], F32).reshape(1, 1, 2, 1, 1)
    cv = cv.transpose(0, 3, 1, 2, 4).reshape(n, nc, CONV_W + 1, 2 * fc)
    cv = jnp.pad(cv, ((0, 0), (0, 0), (0, SUBLANES - CONV_W - 1), (0, 0)))
    return dict(wup=wup, cv=cv, wdn=ffn_w_down.reshape(n, nc, fc, d).astype(BF16),
                wproj=ple_w_proj.astype(BF16), wgate=ple_w_gate.astype(BF16))


def _ffn_ple(x, g_all, layer, ff, p_all, batch, mix=None):
    s, d = x.shape
    _, nc, fc, _ = ff["wdn"].shape
    tm = TM_FFN
    pdim = p_all.shape[-1]

    row = lambda i: (i, 0)
    args, specs = [x], [pl.BlockSpec((tm, d), row)]
    if mix is not None:
        ya, yb, wo_all, e_idx = mix
        args += [ya, yb, wo_all]
        specs += [pl.BlockSpec((tm, ya.shape[1]), row), pl.BlockSpec((tm, yb.shape[1]), row),
                  _layer_spec(wo_all.shape, e_idx)]
    args += [g_all, ff["wup"], ff["cv"], ff["wdn"], p_all, ff["wproj"], ff["wgate"]]
    specs += [_layer_spec(g_all.shape, layer), _layer_spec(ff["wup"].shape, layer),
              _layer_spec(ff["cv"].shape, layer), _layer_spec(ff["wdn"].shape, layer),
              pl.BlockSpec((None, None, tm, pdim), lambda i: (layer, batch, i, 0)),
              _layer_spec(ff["wproj"].shape, layer), _layer_spec(ff["wgate"].shape, layer)]
    kern = functools.partial(_ffn_ple_kernel, tm=tm, d_model=d, n_chunks=nc,
                             with_mix=mix is not None)
    return pl.pallas_call(
        kern,
        grid=(s // tm,),
        in_specs=specs,
        out_specs=pl.BlockSpec((tm, d), row),
        out_shape=jax.ShapeDtypeStruct((s, d), F32),
        scratch_shapes=[pltpu.VMEM((nc, SUBLANES, 2 * fc), F32),
                        pltpu.VMEM((tm, d), F32),
                        pltpu.VMEM((tm, d), BF16),
                        pltpu.VMEM((tm + SUBLANES, 2 * fc), F32),
                        pltpu.VMEM((tm + SUBLANES, 2 * fc), F32)],
        compiler_params=pltpu.CompilerParams(
            dimension_semantics=("arbitrary",), vmem_limit_bytes=VMEM_LIMIT),
        name="ffn_ple_mix" if mix is not None else "ffn_ple",
    )(*args)


def _s5_prep_kernel(are_ref, aim_ref, ldt_ref, bre_ref, bim_ref,
                    abre_ref, abim_ref, bbre_ref, bbim_ref):
    lr, li = are_ref[...], aim_ref[...]
    dt = jnp.exp(ldt_ref[...])
    mag = jnp.exp(lr * dt)
    ab_re = mag * jnp.cos(li * dt)
    ab_im = mag * jnp.sin(li * dt)
    den = lr * lr + li * li
    nr, ni = ab_re - 1.0, ab_im
    cr = (nr * lr + ni * li) / den
    ci = (ni * lr - nr * li) / den
    br, bi = bre_ref[...], bim_ref[...]
    abre_ref[...] = ab_re
    abim_ref[...] = ab_im
    bbre_ref[...] = cr * br - ci * bi
    bbim_ref[...] = cr * bi + ci * br


def _prep_odd(od_a_re, od_a_im, od_log_dt, od_b_re, od_b_im, od_c_re, od_c_im, od_w_glu):
    n, g, pst, c = od_b_re.shape
    shape2 = (n * g * pst * c // LANES, LANES)
    full = (n, g, pst, c)
    rep = lambda a: jnp.broadcast_to(a[..., None], full).reshape(shape2)
    ldt = jnp.broadcast_to(od_log_dt[:, :, None, None], full).reshape(shape2)
    out = jax.ShapeDtypeStruct(shape2, F32)
    ab_re, ab_im, bb_re, bb_im = pl.pallas_call(
        _s5_prep_kernel, out_shape=[out] * 4, name="s5_prep",
    )(rep(od_a_re), rep(od_a_im), ldt, od_b_re.reshape(shape2), od_b_im.reshape(shape2))
    ab_re = ab_re.reshape(full)[..., 0]
    ab_im = ab_im.reshape(full)[..., 0]

    gps = LANES // S5_GROUP_CH
    n_slab = g // gps
    slab_states = gps * pst
    eye = jnp.eye(gps, dtype=BF16)

    def bmat(bb):
        bb = bb.astype(BF16).reshape(n, n_slab, gps, pst, c)
        w = jnp.einsum('njgpc,gh->njgchp', bb, eye)
        return w.reshape(n, n_slab, LANES, slab_states)

    def cmat(cc):
        cc = cc.astype(BF16).reshape(n, n_slab, gps, c, pst)
        w = jnp.einsum('njgcp,gh->njgphc', cc, eye)
        return w.reshape(n, n_slab, slab_states, LANES)

    a = jnp.stack([ab_re.reshape(n, n_slab, slab_states), ab_im.reshape(n, n_slab, slab_states)], axis=1)
    return dict(a=a, wbre=bmat(bb_re.reshape(full)), wbim=bmat(bb_im.reshape(full)),
                cre=cmat(od_c_re), cim=cmat(od_c_im), wglu=od_w_glu.astype(BF16))


def _s5_kernel(x_ref, g_ref, wbre_ref, wbim_ref, a_ref, cre_ref, cim_ref, d_ref, wglu_ref,
               o_ref, bre_scr, bim_scr, st_scr, *, t_tile, pitch, d_model, n_slab, n_lb):
    i = pl.program_id(0)

    @pl.when(i == 0)
    def _():
        st_scr[...] = jnp.zeros_like(st_scr)

    xt = x_ref[...]
    h = _rms(xt, d_model) * g_ref[...]
    rows8 = lax.broadcasted_iota(jnp.int32, (SUBLANES, LANES), 0)

    for j in range(n_slab):
        base = j * pitch
        sh = base % SUBLANES
        hj = h[:, j * LANES:(j + 1) * LANES]
        if sh:
            hj = pltpu.roll(hj, sh, 0)
        lhs = hj.astype(BF16)
        for w_ref, scr in ((wbre_ref, bre_scr), (wbim_ref, bim_scr)):
            bu = _dot(lhs, w_ref[j])
            for k in range(n_lb):
                blk = bu[:, k * LANES:(k + 1) * LANES]
                scr[k, pl.ds(base - sh, t_tile), :] = blk
                if sh:
                    scr[k, pl.ds(base - sh + t_tile, SUBLANES), :] = blk[:SUBLANES]

    ar = [a_ref[0, :, k * LANES:(k + 1) * LANES] for k in range(n_lb)]
    ai = [a_ref[1, :, k * LANES:(k + 1) * LANES] for k in range(n_lb)]

    def body(t, carry):
        xr, xi = carry
        nxr, nxi = [], []
        for k in range(n_lb):
            idx = (k, pl.ds(t, SUBLANES, stride=pitch), slice(None))
            nr = ar[k] * xr[k] - ai[k] * xi[k] + bre_scr[idx]
            ni = ar[k] * xi[k] + ai[k] * xr[k] + bim_scr[idx]
            bre_scr[idx] = nr
            bim_scr[idx] = ni
            nxr.append(nr)
            nxi.append(ni)
        return tuple(nxr), tuple(nxi)

    init = (tuple(st_scr[0, k] for k in range(n_lb)), tuple(st_scr[1, k] for k in range(n_lb)))
    xr, xi = lax.fori_loop(0, t_tile, body, init, unroll=SCAN_UNROLL)
    for k in range(n_lb):
        st_scr[0, k] = xr[k]
        st_scr[1, k] = xi[k]

    def slab_states(scr, base, sh):
        blocks = []
        for k in range(n_lb):
            main = scr[k, pl.ds(base - sh, t_tile), :]
            if sh:
                tail = scr[k, pl.ds(base - sh + t_tile, SUBLANES), :]
                first = jnp.where(rows8 < sh, tail, main[:SUBLANES])
                main = jnp.concatenate([first, main[SUBLANES:]], axis=0)
            blocks.append(main)
        return jnp.concatenate(blocks, axis=-1).astype(BF16)

    ys = []
    for j in range(n_slab):
        base = j * pitch
        sh = base % SUBLANES
        yj = (_dot(slab_states(bre_scr, base, sh), cre_ref[j])
              - _dot(slab_states(bim_scr, base, sh), cim_ref[j]))
        if sh:
            yj = pltpu.roll(yj, t_tile - sh, 0)
        ys.append(yj)
    y = jnp.concatenate(ys, axis=-1) + d_ref[...] * h
    gl = _dot(jax.nn.gelu(y).astype(BF16), wglu_ref[...])
    o_ref[...] = xt + gl[:, :d_model] * jax.nn.sigmoid(gl[:, d_model:])


def _s5_layer(x, g_all, layer, od, o_idx, d_all):
    s, d = x.shape
    _, n_slab, _, slab_states = od["wbre"].shape
    assert n_slab == SUBLANES
    n_lb = slab_states // LANES
    t_tile, pitch = T_S5, S5_PITCH

    row = lambda i: (i, 0)
    kern = functools.partial(_s5_kernel, t_tile=t_tile, pitch=pitch, d_model=d,
                             n_slab=n_slab, n_lb=n_lb)
    return pl.pallas_call(
        kern,
        grid=(s // t_tile,),
        in_specs=[
            pl.BlockSpec((t_tile, d), row),
            _layer_spec(g_all.shape, layer),
            _layer_spec(od["wbre"].shape, o_idx), _layer_spec(od["wbim"].shape, o_idx),
            _layer_spec(od["a"].shape, o_idx),
            _layer_spec(od["cre"].shape, o_idx), _layer_spec(od["cim"].shape, o_idx),
            _layer_spec(d_all.shape, o_idx), _layer_spec(od["wglu"].shape, o_idx),
        ],
        out_specs=pl.BlockSpec((t_tile, d), row),
        out_shape=jax.ShapeDtypeStruct((s, d), F32),
        scratch_shapes=[pltpu.VMEM((n_lb, n_slab * pitch, LANES), F32),
                        pltpu.VMEM((n_lb, n_slab * pitch, LANES), F32),
                        pltpu.VMEM((2, n_lb, SUBLANES, LANES), F32)],
        compiler_params=pltpu.CompilerParams(
            dimension_semantics=("arbitrary",), vmem_limit_bytes=VMEM_LIMIT),
        name="s5_layer",
    )(x, g_all, od["wbre"], od["wbim"], od["a"], od["cre"], od["cim"], d_all, od["wglu"])


def kernel(x, p, norm_mix, norm_ffn, ev_w_in, ev_b_fgate, ev_q_norm, ev_k_norm, ev_v_norm,
           ev_w_spatial, ev_b_spatial, ev_w_out, od_a_re, od_a_im, od_log_dt, od_b_re, od_b_im,
           od_c_re, od_c_im, od_d, od_w_glu, ffn_w_up, ffn_conv_w, ffn_conv_b, ffn_w_down,
           ple_w_proj, ple_w_gate):
    bsz, seq, d = x.shape
    depth = p.shape[0]
    ev = _prep_even(ev_w_in, ev_b_fgate, ev_q_norm, ev_k_norm, ev_v_norm, ev_b_spatial, ev_w_out)
    od = _prep_odd(od_a_re, od_a_im, od_log_dt, od_b_re, od_b_im, od_c_re, od_c_im, od_w_glu)
    ff = _prep_ffn(ffn_w_up, ffn_conv_w, ffn_conv_b, ffn_w_down, ple_w_proj, ple_w_gate)
    g_mix = norm_mix.reshape(depth, 1, d)
    g_ffn = norm_ffn.reshape(depth, 1, d)
    d_skip = od_d.reshape(-1, 1, d)
    outs = []
    for b in range(bsz):
        xb = x[b]
        for i in range(depth):
            mix = None
            if i % 2 == 0:
                e = i // 2
                ya, qt, k, vt = _even_pre(xb, g_mix, i, ev, e, ev_w_spatial)
                mix = (ya, _attention(qt, k, vt), ev["wo"], e)
            else:
                xb = _s5_layer(xb, g_mix, i, od, i // 2, d_skip)
            xb = _ffn_ple(xb, g_ffn, i, ff, p, b, mix=mix)
        outs.append(xb)
    return outs[0][None] if bsz == 1 else jnp.stack(outs)
```

```python
import functools

import jax
import jax.numpy as jnp
import numpy as np
from jax import lax
from jax.experimental import pallas as pl
from jax.experimental.pallas import tpu as pltpu

F32 = jnp.float32
BF16 = jnp.bfloat16
EPS = 1e-6
NEG_INF = -1e30
LANES = 128
SUBLANES = 8
VMEM_LIMIT = 56 * 1024 * 1024

A_GROUPS = 4
A_CHUNK = 128
N_HEADS = 8
HEAD_DIM = 64
S5_GROUP_CH = 16
S5_STATE = 64
CONV_W = 3

LOG2E = 1.4426950408889634
V_ROWS = 80

TILE_ATT = 512
TQ_ATT = 1024
SUB_K = 256
TM_FFN = 512
FF_CHUNK = 256
T_S5 = 512
S5_PITCH = T_S5 + 4
SCAN_UNROLL = 8


def _const_spec(shape):
    n = len(shape)
    return pl.BlockSpec(shape, lambda *_: (0,) * n, pipeline_mode=pl.Buffered(1))


def _layer_spec(shape, layer):
    n = len(shape) - 1
    return pl.BlockSpec((None,) + tuple(shape[1:]), lambda *_: (layer,) + (0,) * n,
                        pipeline_mode=pl.Buffered(1))


def _rms(x, n):
    return x * lax.rsqrt(jnp.sum(x * x, axis=-1, keepdims=True) * (1.0 / n) + EPS)


def _dot(a, b):
    return jnp.dot(a, b, preferred_element_type=F32)


def _sigmoid(x):
    return 0.5 + 0.5 * jnp.tanh(0.5 * x)


def _split3(x):
    hi = x.astype(BF16)
    r = x - hi.astype(F32)
    mid = r.astype(BF16)
    lo = (r - mid.astype(F32)).astype(BF16)
    return hi, mid, lo


def _even_pre_kernel(x_ref, g_ref, wuv_ref, wq_ref, wk_ref, wv_ref, wf_ref, bf_ref,
                     qg_ref, kg_ref, kmask_ref, qc_ref, vc_ref, vgain_ref, wsp_ref, bsp_ref, e_ref,
                     ya_ref, qt_ref, k_ref, vt_ref, carry_ref, *, tm, d_model, a_width):
    i = pl.program_id(0)

    @pl.when(i == 0)
    def _():
        carry_ref[...] = jnp.zeros_like(carry_ref)

    h = (_rms(x_ref[...], d_model) * g_ref[...]).astype(BF16)

    zuv = _dot(h, wuv_ref[...])
    u = jax.nn.gelu(zuv[:, :a_width])
    v = jax.nn.gelu(zuv[:, a_width:])
    tri = (lax.broadcasted_iota(jnp.int32, (A_CHUNK, A_CHUNK), 1)
           <= lax.broadcasted_iota(jnp.int32, (A_CHUNK, A_CHUNK), 0))
    for g in range(A_GROUPS):
        cols = slice(g * LANES, (g + 1) * LANES)
        vg = (_rms(v[:, cols], LANES) * vgain_ref[:, cols]).astype(BF16)
        w = jnp.where(tri, wsp_ref[g], 0.0).astype(BF16)
        n_ch = tm // A_CHUNK
        vcat = jnp.concatenate([vg[c * A_CHUNK:(c + 1) * A_CHUNK] for c in range(n_ch)], axis=1)
        sv = _dot(w, vcat)
        for c in range(n_ch):
            rows = slice(c * A_CHUNK, (c + 1) * A_CHUNK)
            ya_ref[rows, cols] = (u[rows, cols] * (sv[:, c * LANES:(c + 1) * LANES] + bsp_ref[g])).astype(BF16)

    f = _dot(h, wf_ref[...]) + bf_ref[...]
    ls = jnp.minimum(f, 0.0) - jnp.log1p(jnp.exp(-jnp.abs(f)))
    tri_t = (lax.broadcasted_iota(jnp.int32, (tm, tm), 1)
             <= lax.broadcasted_iota(jnp.int32, (tm, tm), 0)).astype(BF16)
    c3 = _dot(tri_t, jnp.concatenate(_split3(ls), axis=1))
    cum = c3[:, :LANES] + c3[:, LANES:2 * LANES] + c3[:, 2 * LANES:] + carry_ref[0:1, :]
    carry_ref[...] = jnp.broadcast_to(cum[tm - 1:tm, :], carry_ref.shape)
    kadd = _dot(jnp.concatenate(_split3(cum * LOG2E), axis=1), e_ref[...])

    zqt = _dot(h, wq_ref[...]).T
    zvt = _dot(h, wv_ref[...]).T
    zk = _dot(h, wk_ref[...])
    for hd in range(N_HEADS):
        rows = slice(hd * HEAD_DIM, (hd + 1) * HEAD_DIM)
        qh = zqt[rows]
        qn = qh * lax.rsqrt(jnp.sum(qh * qh, axis=0, keepdims=True) * (1.0 / HEAD_DIM) + EPS)
        qt_ref[hd, 0] = jnp.concatenate([qn * qg_ref[...], qc_ref[...]], axis=0).astype(BF16)
        vt_ref[hd, 0] = jnp.concatenate([zvt[rows], vc_ref[...]], axis=0).astype(BF16)
        kb = zk[:, (hd // 2) * LANES:(hd // 2 + 1) * LANES]
        if hd % 2:
            kb = pltpu.roll(kb, HEAD_DIM, 1)
        kb = kb * kmask_ref[...]
        k_ref[hd] = (_rms(kb, HEAD_DIM) * kg_ref[...] + kadd[:, hd * LANES:(hd + 1) * LANES]).astype(BF16)


def _prep_even(ev_w_in, ev_b_fgate, ev_q_norm, ev_k_norm, ev_v_norm, ev_b_spatial, ev_w_out):
    n, d, _ = ev_w_in.shape
    a_width = A_GROUPS * LANES
    b_width = N_HEADS * HEAD_DIM
    o = 2 * a_width
    w_in = ev_w_in.astype(BF16)
    wf = jnp.pad(w_in[:, :, o + 3 * b_width:], ((0, 0), (0, 0), (0, LANES - N_HEADS)))
    bf = jnp.pad(ev_b_fgate, ((0, 0), (0, LANES - N_HEADS))).reshape(n, 1, LANES)
    qg = jnp.broadcast_to((ev_q_norm * (HEAD_DIM ** -0.5 * LOG2E))[:, :, None], (n, HEAD_DIM, TILE_ATT))
    kg = jnp.pad(ev_k_norm, ((0, 0), (0, LANES - HEAD_DIM))).reshape(n, 1, LANES)
    bsp = jnp.broadcast_to(ev_b_spatial[..., None], ev_b_spatial.shape + (LANES,))
    return dict(w_in=w_in, wf=wf, bf=bf, qg=qg, kg=kg,
                vgain=ev_v_norm.reshape(n, 1, a_width), bsp=bsp, wo=ev_w_out.astype(BF16))


def _even_consts():
    row = np.arange(LANES - HEAD_DIM)[:, None]
    qc = np.broadcast_to(np.where(row < 3, -1.0, 0.0), (LANES - HEAD_DIM, TILE_ATT)).astype(np.float32)
    vc = np.broadcast_to(np.where(row[:V_ROWS - HEAD_DIM] == 0, 1.0, 0.0),
                         (V_ROWS - HEAD_DIM, TILE_ATT)).astype(np.float32)
    kmask = (np.arange(LANES) < HEAD_DIM).astype(np.float32).reshape(1, LANES)
    e = np.zeros((3 * LANES, N_HEADS * LANES), np.float32)
    hh = np.arange(N_HEADS)
    for part in range(3):
        e[part * LANES + hh, hh * LANES + HEAD_DIM + part] = 1.0
    return jnp.asarray(qc), jnp.asarray(vc), jnp.asarray(kmask), jnp.asarray(e, dtype=BF16)


def _even_pre(x, g_all, layer, ev, e_idx, w_sp_all):
    s, d = x.shape
    a_width = A_GROUPS * LANES
    b_width = N_HEADS * HEAD_DIM
    tm = TILE_ATT
    qc, vc, kmask, e = _even_consts()
    assert (2 * a_width) % b_width == 0
    q_blk = 2 * a_width // b_width
    wcols = lambda width, j: pl.BlockSpec((None, d, width), lambda i: (e_idx, 0, j),
                                          pipeline_mode=pl.Buffered(1))
    row = lambda i: (i, 0)
    hrow = lambda i: (0, i, 0)
    trow = lambda i: (0, i, 0, 0)
    qsplit = TQ_ATT // tm
    kern = functools.partial(_even_pre_kernel, tm=tm, d_model=d, a_width=a_width)
    return pl.pallas_call(
        kern,
        grid=(s // tm,),
        in_specs=[
            pl.BlockSpec((tm, d), row),
            _layer_spec(g_all.shape, layer),
            wcols(2 * a_width, 0), wcols(b_width, q_blk), wcols(b_width, q_blk + 1),
            wcols(b_width, q_blk + 2),
            _layer_spec(ev["wf"].shape, e_idx), _layer_spec(ev["bf"].shape, e_idx),
            _layer_spec(ev["qg"].shape, e_idx), _layer_spec(ev["kg"].shape, e_idx),
            _const_spec(kmask.shape), _const_spec(qc.shape), _const_spec(vc.shape),
            _layer_spec(ev["vgain"].shape, e_idx), _layer_spec(w_sp_all.shape, e_idx),
            _layer_spec(ev["bsp"].shape, e_idx), _const_spec(e.shape),
        ],
        out_specs=[
            pl.BlockSpec((tm, a_width), row),
            pl.BlockSpec((N_HEADS, 1, LANES, tm), lambda i: (0, i // qsplit, 0, i % qsplit)),
            pl.BlockSpec((N_HEADS, tm, LANES), hrow),
            pl.BlockSpec((N_HEADS, 1, V_ROWS, tm), trow),
        ],
        out_shape=[jax.ShapeDtypeStruct((s, a_width), BF16),
                   jax.ShapeDtypeStruct((N_HEADS, s // TQ_ATT, LANES, TQ_ATT), BF16),
                   jax.ShapeDtypeStruct((N_HEADS, s, LANES), BF16),
                   jax.ShapeDtypeStruct((N_HEADS, s // tm, V_ROWS, tm), BF16)],
        scratch_shapes=[pltpu.VMEM((SUBLANES, LANES), F32)],
        compiler_params=pltpu.CompilerParams(
            dimension_semantics=("arbitrary",), vmem_limit_bytes=VMEM_LIMIT),
        name="even_pre",
    )(x, g_all, ev["w_in"], ev["w_in"], ev["w_in"], ev["w_in"], ev["wf"], ev["bf"], ev["qg"], ev["kg"],
      kmask, qc, vc, ev["vgain"], w_sp_all, ev["bsp"], e)


def _attention_kernel(qt_ref, k_ref, vt_ref, o_ref, s0_ref, s1_ref, mx0_ref, mx1_ref, *, tq, tk):
    qi = pl.program_id(1)
    krow = lax.broadcasted_iota(jnp.int32, (SUB_K, tq), 0)
    qcol = lax.broadcasted_iota(jnp.int32, (SUB_K, tq), 1)
    n_sub = tk // SUB_K

    pieces = [(sub, hh) for sub in range(n_sub) for hh in range(2)]

    def scores(buf, kv, sub, hh, q0=0):
        s_ref, mx_ref = buf
        start = pl.multiple_of(kv * tk, tk)
        kc = k_ref[hh, pl.ds(start + sub * SUB_K, SUB_K), :]
        if q0:
            s_ref[hh, sub * SUB_K:(sub + 1) * SUB_K, q0:] = _dot(kc, qt_ref[hh, 0][:, q0:])
            return
        st = _dot(kc, qt_ref[hh, 0])
        s_ref[hh, sub * SUB_K:(sub + 1) * SUB_K, :] = st
        mx_ref[sub * 2 + hh] = jnp.max(st, axis=0, keepdims=True)

    def rescale(buf, state, hh):
        _, mx_ref = buf
        m, acc = state[hh]
        m_new = m
        for sub in range(n_sub):
            m_new = jnp.maximum(m_new, mx_ref[sub * 2 + hh])
        state[hh] = (m_new, acc * jnp.exp2(m - m_new))

    def softmax_pv(buf, kv, state, diag, sub, hh):
        s_ref, _ = buf
        m_all, acc_all = state[hh]
        q0 = 0 if diag is None else diag * tk
        m, acc = m_all[:, q0:], acc_all[:, q0:]
        st = s_ref[hh, sub * SUB_K:(sub + 1) * SUB_K, q0:]
        if diag is not None:
            kpos = lax.broadcasted_iota(jnp.int32, st.shape, 0) + (q0 + sub * SUB_K)
            qpos = lax.broadcasted_iota(jnp.int32, st.shape, 1) + q0
            st = jnp.where(kpos <= qpos, st, NEG_INF)
            m_new = jnp.maximum(m, jnp.max(st, axis=0, keepdims=True))
            acc = acc * jnp.exp2(m - m_new)
            m = m_new
        pt = jnp.exp2(st - m).astype(BF16)
        vt = vt_ref[hh, kv][:, sub * SUB_K:(sub + 1) * SUB_K]
        acc = acc + _dot(vt, pt)
        if q0:
            m = jnp.concatenate([m_all[:, :q0], m], axis=1)
            acc = jnp.concatenate([acc_all[:, :q0], acc], axis=1)
        state[hh] = (m, acc)

    def stage(state, cur, cur_kv, diag, nxt=None, nxt_kv=None, nxt_q0=0):
        state = list(state)
        if diag is None:
            for hh in range(2):
                rescale(cur, state, hh)
        for sub, hh in pieces:
            if nxt is not None:
                scores(nxt, nxt_kv, sub, hh, nxt_q0)
            softmax_pv(cur, cur_kv, state, diag, sub, hh)
        return tuple(state)

    buf0, buf1 = (s0_ref, mx0_ref), (s1_ref, mx1_ref)

    def pair(jj, state):
        c0 = 2 * jj
        state = stage(state, buf0, c0, None, buf1, c0 + 1)
        return stage(state, buf1, c0 + 1, None, buf0, c0 + 2)

    init = tuple((jnp.full((1, tq), NEG_INF, F32), jnp.zeros((V_ROWS, tq), F32)) for _ in range(2))
    for sub, hh in pieces:
        scores(buf0, 0, sub, hh)
    state = lax.fori_loop(0, qi, pair, init)
    state = stage(state, buf0, 2 * qi, 0, buf1, 2 * qi + 1, nxt_q0=tk)
    state = stage(state, buf1, 2 * qi + 1, 1)
    ot = jnp.concatenate([acc[:HEAD_DIM] / acc[HEAD_DIM:HEAD_DIM + 1] for _, acc in state], axis=0)
    o_ref[...] = ot.T.astype(o_ref.dtype)


def _attention(qt, k, vt):
    _, s, _ = k.shape
    tq, tk = TQ_ATT, TILE_ATT
    assert tq == 2 * tk
    nk = s // tk
    kern = functools.partial(_attention_kernel, tq=tq, tk=tk)
    return pl.pallas_call(
        kern,
        grid=(N_HEADS // 2, s // tq),
        in_specs=[
            pl.BlockSpec((2, 1, LANES, tq), lambda hp, qi: (hp, qi, 0, 0)),
            pl.BlockSpec((2, s, LANES), lambda hp, qi: (hp, 0, 0)),
            pl.BlockSpec((2, nk, V_ROWS, tk), lambda hp, qi: (hp, 0, 0, 0)),
        ],
        out_specs=pl.BlockSpec((tq, 2 * HEAD_DIM), lambda hp, qi: (qi, hp)),
        out_shape=jax.ShapeDtypeStruct((s, N_HEADS * HEAD_DIM), BF16),
        scratch_shapes=[pltpu.VMEM((2, tk, tq), F32), pltpu.VMEM((2, tk, tq), F32),
                        pltpu.VMEM((2 * tk // SUB_K, 1, tq), F32),
                        pltpu.VMEM((2 * tk // SUB_K, 1, tq), F32)],
        compiler_params=pltpu.CompilerParams(
            dimension_semantics=("arbitrary", "arbitrary"), vmem_limit_bytes=VMEM_LIMIT),
        name="attention",
    )(qt, k, vt)


def _ffn_ple_kernel(*refs, tm, d_model, n_chunks, with_mix):
    if with_mix:
        (x_ref, ya_ref, yb_ref, wo_ref, g_ref, wup_ref, cv_ref, wdn_ref, p_ref,
         wproj_ref, wgate_ref, o_ref, halo_ref, acc_ref, xn_ref, h0_ref, h1_ref) = refs
    else:
        (x_ref, g_ref, wup_ref, cv_ref, wdn_ref, p_ref,
         wproj_ref, wgate_ref, o_ref, halo_ref, acc_ref, xn_ref, h0_ref, h1_ref) = refs
    i = pl.program_id(0)

    @pl.when(i == 0)
    def _():
        halo_ref[...] = jnp.zeros_like(halo_ref)

    x1 = x_ref[...]
    if with_mix:
        x1 = x1 + _dot(jnp.concatenate([ya_ref[...], yb_ref[...]], axis=1), wo_ref[...])
    xn_ref[...] = (_rms(x1, d_model) * g_ref[...]).astype(BF16)
    acc_ref[...] = x1

    fc = FF_CHUNK

    d_ff = n_chunks * fc

    def up_proj(h_ref, c):
        col = pl.multiple_of(c * fc, fc)
        xn = xn_ref[...]
        h_ref[SUBLANES:, :fc] = _dot(xn, wup_ref[:, pl.ds(col, fc)])
        h_ref[SUBLANES:, fc:] = _dot(xn, wup_ref[:, pl.ds(d_ff + col, fc)])

    def process(h_ref, c):
        h_ref[:SUBLANES, :] = halo_ref[c]
        halo_ref[c] = h_ref[tm:, :]
        cv = cv_ref[c]
        hc = (cv[3:4] + cv[0:1] * h_ref[SUBLANES - 2:tm + SUBLANES - 2, :]
              + cv[1:2] * h_ref[SUBLANES - 1:tm + SUBLANES - 1, :] + cv[2:3] * h_ref[SUBLANES:, :])
        half_gate, up = hc[:, :fc], hc[:, fc:]
        act = ((half_gate + half_gate * jnp.tanh(half_gate)) * up).astype(BF16)
        acc_ref[...] += _dot(act, wdn_ref[c])

    def pair(jj, carry):
        c0 = 2 * jj
        up_proj(h1_ref, c0 + 1)
        process(h0_ref, c0)
        up_proj(h0_ref, c0 + 2)
        process(h1_ref, c0 + 1)
        return carry

    n_pairs = (n_chunks - 1) // 2
    up_proj(h0_ref, 0)
    lax.fori_loop(0, n_pairs, pair, 0)
    if n_chunks - 2 * n_pairs == 2:
        up_proj(h1_ref, n_chunks - 1)
        process(h0_ref, n_chunks - 2)
        process(h1_ref, n_chunks - 1)
    else:
        process(h0_ref, n_chunks - 1)

    x2 = acc_ref[...]
    gate = _sigmoid(_dot(_rms(x2, d_model).astype(BF16), wgate_ref[...]))
    o_ref[...] = x2 + gate * _dot(p_ref[...].astype(BF16), wproj_ref[...])


def _prep_ffn(ffn_w_up, ffn_conv_w, ffn_conv_b, ffn_w_down, ple_w_proj, ple_w_gate):
    n, d, two_ff = ffn_w_up.shape
    d_ff = two_ff // 2
    fc = FF_CHUNK
    nc = d_ff // fc
    wup = ffn_w_up.astype(BF16)
    cv =jnp.concatenate([ffn_conv_w, ffn_conv_b[:, None]], axis=1)
    cv = cv.reshape(n, CONV_W + 1, 2, nc, fc) * jnp.array([0.5, 1.0], F32).reshape(1, 1, 2, 1, 1)
    cv = cv.transpose(0, 3, 1, 2, 4).reshape(n, nc, CONV_W + 1, 2 * fc)
    cv = jnp.pad(cv, ((0, 0), (0, 0), (0, SUBLANES - CONV_W - 1), (0, 0)))
    return dict(wup=wup, cv=cv, wdn=ffn_w_down.reshape(n, nc, fc, d).astype(BF16),
                wproj=ple_w_proj.astype(BF16), wgate=ple_w_gate.astype(BF16))


def _ffn_ple(x, g_all, layer, ff, p_all, batch, mix=None):
    s, d = x.shape
    _, nc, fc, _ = ff["wdn"].shape
    tm = TM_FFN
    pdim = p_all.shape[-1]

    row = lambda i: (i, 0)
    args, specs = [x], [pl.BlockSpec((tm, d), row)]
    if mix is not None:
        ya, yb, wo_all, e_idx = mix
        args += [ya, yb, wo_all]
        specs += [pl.BlockSpec((tm, ya.shape[1]), row), pl.BlockSpec((tm, yb.shape[1]), row),
                  _layer_spec(wo_all.shape, e_idx)]
    args += [g_all, ff["wup"], ff["cv"], ff["wdn"], p_all, ff["wproj"], ff["wgate"]]
    specs += [_layer_spec(g_all.shape, layer), _layer_spec(ff["wup"].shape, layer),
              _layer_spec(ff["cv"].shape, layer), _layer_spec(ff["wdn"].shape, layer),
              pl.BlockSpec((None, None, tm, pdim), lambda i: (layer, batch, i, 0)),
              _layer_spec(ff["wproj"].shape, layer), _layer_spec(ff["wgate"].shape, layer)]
    kern = functools.partial(_ffn_ple_kernel, tm=tm, d_model=d, n_chunks=nc,
                             with_mix=mix is not None)
    return pl.pallas_call(
        kern,
        grid=(s // tm,),
        in_specs=specs,
        out_specs=pl.BlockSpec((tm, d), row),
        out_shape=jax.ShapeDtypeStruct((s, d), F32),
        scratch_shapes=[pltpu.VMEM((nc, SUBLANES, 2 * fc), F32),
                        pltpu.VMEM((tm, d), F32),
                        pltpu.VMEM((tm, d), BF16),
                        pltpu.VMEM((tm + SUBLANES, 2 * fc), F32),
                        pltpu.VMEM((tm + SUBLANES, 2 * fc), F32)],
        compiler_params=pltpu.CompilerParams(
            dimension_semantics=("arbitrary",), vmem_limit_bytes=VMEM_LIMIT),
        name="ffn_ple_mix" if mix is not None else "ffn_ple",
    )(*args)


def _s5_prep_kernel(are_ref, aim_ref, ldt_ref, bre_ref, bim_ref,
                    abre_ref, abim_ref, bbre_ref, bbim_ref):
    lr, li = are_ref[...], aim_ref[...]
    dt = jnp.exp(ldt_ref[...])
    mag = jnp.exp(lr * dt)
    ab_re = mag * jnp.cos(li * dt)
    ab_im = mag * jnp.sin(li * dt)
    den = lr * lr + li * li
    nr, ni = ab_re - 1.0, ab_im
    cr = (nr * lr + ni * li) / den
    ci = (ni * lr - nr * li) / den
    br, bi = bre_ref[...], bim_ref[...]
    abre_ref[...] = ab_re
    abim_ref[...] = ab_im
    bbre_ref[...] = cr * br - ci * bi
    bbim_ref[...] = cr * bi + ci * br


def _prep_odd(od_a_re, od_a_im, od_log_dt, od_b_re, od_b_im, od_c_re, od_c_im, od_w_glu):
    n, g, pst, c = od_b_re.shape
    shape2 = (n * g * pst * c // LANES, LANES)
    full = (n, g, pst, c)
    rep = lambda a: jnp.broadcast_to(a[..., None], full).reshape(shape2)
    ldt = jnp.broadcast_to(od_log_dt[:, :, None, None], full).reshape(shape2)
    out = jax.ShapeDtypeStruct(shape2, F32)
    ab_re, ab_im, bb_re, bb_im = pl.pallas_call(
        _s5_prep_kernel, out_shape=[out] * 4, name="s5_prep",
    )(rep(od_a_re), rep(od_a_im), ldt, od_b_re.reshape(shape2), od_b_im.reshape(shape2))
    ab_re = ab_re.reshape(full)[..., 0]
    ab_im = ab_im.reshape(full)[..., 0]

    gps = LANES // S5_GROUP_CH
    n_slab = g // gps
    slab_states = gps * pst
    eye = jnp.eye(gps, dtype=BF16)

    def bmat(bb):
        bb = bb.astype(BF16).reshape(n, n_slab, gps, pst, c)
        w = jnp.einsum('njgpc,gh->njgchp', bb, eye)
        return w.reshape(n, n_slab, LANES, slab_states)

    def cmat(cc):
        cc = cc.astype(BF16).reshape(n, n_slab, gps, c, pst)
        w = jnp.einsum('njgcp,gh->njgphc', cc, eye)
        return w.reshape(n, n_slab, slab_states, LANES)

    a = jnp.stack([ab_re.reshape(n, n_slab, slab_states), ab_im.reshape(n, n_slab, slab_states)], axis=1)
    return dict(a=a, wbre=bmat(bb_re.reshape(full)), wbim=bmat(bb_im.reshape(full)),
                cre=cmat(od_c_re), cim=cmat(od_c_im), wglu=od_w_glu.astype(BF16))


def _s5_kernel(x_ref, g_ref, wbre_ref, wbim_ref, a_ref, cre_ref, cim_ref, d_ref, wglu_ref,
               o_ref, bre_scr, bim_scr, st_scr, *, t_tile, pitch, d_model, n_slab, n_lb):
    i = pl.program_id(0)

    @pl.when(i == 0)
    def _():
        st_scr[...] = jnp.zeros_like(st_scr)

    xt = x_ref[...]
    h = _rms(xt, d_model) * g_ref[...]
    rows8 = lax.broadcasted_iota(jnp.int32, (SUBLANES, LANES), 0)

    for j in range(n_slab):
        base = j * pitch
        sh = base % SUBLANES
        hj = h[:, j * LANES:(j + 1) * LANES]
        if sh:
            hj = pltpu.roll(hj, sh, 0)
        lhs = hj.astype(BF16)
        for w_ref, scr in ((wbre_ref, bre_scr), (wbim_ref, bim_scr)):
            bu = _dot(lhs, w_ref[j])
            for k in range(n_lb):
                blk = bu[:, k * LANES:(k + 1) * LANES]
                scr[k, pl.ds(base - sh, t_tile), :] = blk
                if sh:
                    scr[k, pl.ds(base - sh + t_tile, SUBLANES), :] = blk[:SUBLANES]

    ar = [a_ref[0, :, k * LANES:(k + 1) * LANES] for k in range(n_lb)]
    ai = [a_ref[1, :, k * LANES:(k + 1) * LANES] for k in range(n_lb)]

    def body(t, carry):
        xr, xi = carry
        nxr, nxi = [], []
        for k in range(n_lb):
            idx = (k, pl.ds(t, SUBLANES, stride=pitch), slice(None))
            nr = ar[k] * xr[k] - ai[k] * xi[k] + bre_scr[idx]
            ni = ar[k] * xi[k] + ai[k] * xr[k] + bim_scr[idx]
            bre_scr[idx] = nr
            bim_scr[idx] = ni
            nxr.append(nr)
            nxi.append(ni)
        return tuple(nxr), tuple(nxi)

    init = (tuple(st_scr[0, k] for k in range(n_lb)), tuple(st_scr[1, k] for k in range(n_lb)))
    xr, xi = lax.fori_loop(0, t_tile, body, init, unroll=SCAN_UNROLL)
    for k in range(n_lb):
        st_scr[0, k] = xr[k]
        st_scr[1, k] = xi[k]

    def slab_states(scr, base, sh):
        blocks = []
        for k in range(n_lb):
            main = scr[k, pl.ds(base - sh, t_tile), :]
            if sh:
                tail = scr[k, pl.ds(base - sh + t_tile, SUBLANES), :]
                first = jnp.where(rows8 < sh, tail, main[:SUBLANES])
                main = jnp.concatenate([first, main[SUBLANES:]], axis=0)
            blocks.append(main)
        return jnp.concatenate(blocks, axis=-1).astype(BF16)

    ys = []
    for j in range(n_slab):
        base = j * pitch
        sh = base % SUBLANES
        yj = (_dot(slab_states(bre_scr, base, sh), cre_ref[j])
              - _dot(slab_states(bim_scr, base, sh), cim_ref[j]))
        if sh:
            yj = pltpu.roll(yj, t_tile - sh, 0)
        ys.append(yj)
    y = jnp.concatenate(ys, axis=-1) + d_ref[...] * h
    gl = _dot(jax.nn.gelu(y).astype(BF16), wglu_ref[...])
    o_ref[...] = xt + gl[:, :d_model] * _sigmoid(gl[:, d_model:])


def _s5_layer(x, g_all, layer, od, o_idx, d_all):
    s, d = x.shape
    _, n_slab, _, slab_states = od["wbre"].shape
    assert n_slab == SUBLANES
    n_lb = slab_states // LANES
    t_tile, pitch = T_S5, S5_PITCH

    row = lambda i: (i, 0)
    kern = functools.partial(_s5_kernel, t_tile=t_tile, pitch=pitch, d_model=d,
                             n_slab=n_slab, n_lb=n_lb)
    return pl.pallas_call(
        kern,
        grid=(s // t_tile,),
        in_specs=[
            pl.BlockSpec((t_tile, d), row),
            _layer_spec(g_all.shape, layer),
            _layer_spec(od["wbre"].shape, o_idx), _layer_spec(od["wbim"].shape, o_idx),
            _layer_spec(od["a"].shape, o_idx),
            _layer_spec(od["cre"].shape, o_idx), _layer_spec(od["cim"].shape, o_idx),
            _layer_spec(d_all.shape, o_idx), _layer_spec(od["wglu"].shape, o_idx),
        ],
        out_specs=pl.BlockSpec((t_tile, d), row),
        out_shape=jax.ShapeDtypeStruct((s, d), F32),
        scratch_shapes=[pltpu.VMEM((n_lb, n_slab * pitch, LANES), F32),
                        pltpu.VMEM((n_lb, n_slab * pitch, LANES), F32),
                        pltpu.VMEM((2, n_lb, SUBLANES, LANES), F32)],
        compiler_params=pltpu.CompilerParams(
            dimension_semantics=("arbitrary",), vmem_limit_bytes=VMEM_LIMIT),
        name="s5_layer",
    )(x, g_all, od["wbre"], od["wbim"], od["a"], od["cre"], od["cim"], d_all, od["wglu"])


def kernel(x, p, norm_mix, norm_ffn, ev_w_in, ev_b_fgate, ev_q_norm, ev_k_norm, ev_v_norm,
           ev_w_spatial, ev_b_spatial, ev_w_out, od_a_re, od_a_im, od_log_dt, od_b_re, od_b_im,
           od_c_re, od_c_im, od_d, od_w_glu, ffn_w_up, ffn_conv_w, ffn_conv_b, ffn_w_down,
           ple_w_proj, ple_w_gate):
    bsz, seq, d = x.shape
    depth = p.shape[0]
    ev = _prep_even(ev_w_in, ev_b_fgate, ev_q_norm, ev_k_norm, ev_v_norm, ev_b_spatial, ev_w_out)
    od = _prep_odd(od_a_re, od_a_im, od_log_dt, od_b_re, od_b_im, od_c_re, od_c_im, od_w_glu)
    ff = _prep_ffn(ffn_w_up, ffn_conv_w, ffn_conv_b, ffn_w_down, ple_w_proj, ple_w_gate)
    g_mix = norm_mix.reshape(depth, 1, d)
    g_ffn = norm_ffn.reshape(depth, 1, d)
    d_skip = od_d.reshape(-1, 1, d)
    outs = []
    for b in range(bsz):
        xb = x[b]
        for i in range(depth):
            mix = None
            if i % 2 == 0:
                e = i // 2
                ya, qt, k, vt = _even_pre(xb, g_mix, i, ev, e, ev_w_spatial)
                mix = (ya, _attention(qt, k, vt), ev["wo"], e)
            else:
                xb = _s5_layer(xb, g_mix, i, od, i // 2, d_skip)
            xb = _ffn_ple(xb, g_ffn, i, ff, p, b, mix=mix)
        outs.append(xb)
    return outs[0][None] if bsz == 1 else jnp.stack(outs)
```
